```python
import jax, jax.numpy as jnp
from jax import lax
import numpy as np

D_MODEL = 1024
BATCH = 8
SEQ = 2048
DEPTH = 1
DEC_BATCH = 128
DEC_SEQ = 1
PAST_LEN = 16384
PAGE_SIZE = 128

N_META = 16
MIX_W = D_MODEL
LRU_W = MIX_W // 2
LRU_BLOCKS = 8
LRU_BLK = LRU_W // LRU_BLOCKS
LRU_C = 8.0
CONV_K = 4
GLA_DV = MIX_W - LRU_W
GLA_H = 4
GLA_DVH = GLA_DV // GLA_H
GLA_DK = GLA_DV // 2
GLA_DKH = GLA_DK // GLA_H
GLA_RANK = 16
GLA_TAU = 16.0
GLA_CHUNK = 64
IN_COLS = 2 * LRU_W + 2 * GLA_DK + 2 * GLA_DV + GLA_RANK
D_FF = -(-8 * D_MODEL // (3 * 256)) * 256
DEEPNORM_ALPHA = (2.0 * DEPTH) ** 0.25
DEEPNORM_BETA = (8.0 * DEPTH) ** -0.25
LN_EPS = 1e-5
RMS_EPS = 1e-6

kernel_name = "hymba_rglru_gla_deepnorm_step"


def _layer_norm(x, g, b):
    xf = x.astype(jnp.float32)
    mu = jnp.mean(xf, -1, keepdims=True)
    var = jnp.mean(jnp.square(xf - mu), -1, keepdims=True)
    return ((xf - mu) * lax.rsqrt(var + LN_EPS) * g + b).astype(x.dtype)


def _causal_conv(u, buf, w, b):
    full = jnp.concatenate([buf.astype(u.dtype), u], axis=1)
    T = u.shape[1]
    y = b + sum(full[:, j:j + T] * w[j] for j in range(CONV_K))
    return y, full[:, -(CONV_K - 1):]


def _rg_lru(u, h0, ga_w, ga_b, gx_w, gx_b, lam):
    B, T, W = u.shape
    ub = u.reshape(B, T, LRU_BLOCKS, LRU_BLK)
    r = jax.nn.sigmoid(jnp.einsum('btnc,ncd->btnd', ub, ga_w).reshape(B, T, W) + ga_b)
    i = jax.nn.sigmoid(jnp.einsum('btnc,ncd->btnd', ub, gx_w).reshape(B, T, W) + gx_b)
    log_a = (-LRU_C * r * jax.nn.softplus(-lam)).astype(jnp.float32)
    a = jnp.exp(log_a)
    xin = jnp.sqrt(-jnp.expm1(2.0 * log_a)) * (i * u).astype(jnp.float32)

    def step(h, ax):
        a_t, x_t = ax
        h = a_t * h + x_t
        return h, h

    hT, hs = lax.scan(step, h0.astype(jnp.float32), (a.swapaxes(0, 1), xin.swapaxes(0, 1)))
    return hs.swapaxes(0, 1).astype(u.dtype), hT.astype(h0.dtype)


def _gla_chunks(q, k, v, log_a, s0, chunk):
    B, T, H, _ = q.shape
    n = T // chunk

    def to_chunks(t):
        return t.astype(jnp.float32).reshape(B, n, chunk, H, -1).transpose(1, 0, 3, 2, 4)

    mask = jnp.tril(jnp.ones((chunk, chunk), bool))[:, :, None]

    def step(S, inp):
        qc, kc, vc, gc = inp
        b = jnp.cumsum(gc, axis=2)
        diff = b[:, :, :, None, :] - b[:, :, None, :, :]
        decay = jnp.exp(jnp.where(mask, diff, -jnp.inf))
        scores = jnp.einsum('bhtd,bhsd,bhtsd->bhts', qc, kc, decay)
        o = jnp.einsum('bhts,bhsv->bhtv', scores, vc) + jnp.einsum('bhtd,bhdv->bhtv', qc * jnp.exp(b), S)
        b_last = b[:, :, -1:, :]
        S_new = jnp.exp(b_last[:, :, 0, :])[..., None] * S + jnp.einsum(
            'bhsd,bhsv->bhdv', kc * jnp.exp(b_last - b), vc)
        return S_new, o

    S_T, o = lax.scan(step, s0.astype(jnp.float32), (to_chunks(q), to_chunks(k), to_chunks(v), to_chunks(log_a)))
    o = o.transpose(1, 0, 3, 2, 4).reshape(B, T, H, -1)
    return o, S_T


def _layer(h, conv_buf, lru_h0, gla_s0, segments, p):
    (w_in, conv_w, conv_b, ga_w, ga_b, gx_w, gx_b, lam, al_w, al_b, gn_g, w_out,
     ln1_g, ln1_b, wg, wu, wd, ln2_g, ln2_b) = p
    B, T, _ = h.shape
    proj = jnp.einsum('btd,dc->btc', h, w_in)
    i0 = LRU_W
    i1 = i0 + LRU_W
    i2 = i1 + GLA_DK
    i3 = i2 + GLA_DK
    i4 = i3 + GLA_DV
    i5 = i4 + GLA_DV
    x_lru, g_lru = proj[..., :i0], proj[..., i0:i1]
    q, k, v = proj[..., i1:i2], proj[..., i2:i3], proj[..., i3:i4]
    g_gla, a_lr = proj[..., i4:i5], proj[..., i5:]

    u, conv_new = _causal_conv(x_lru, conv_buf, conv_w, conv_b)
    hs, lru_hT = _rg_lru(u, lru_h0, ga_w, ga_b, gx_w, gx_b, lam)
    y_lru = hs * jax.nn.gelu(g_lru)

    q = q.reshape(B, T, GLA_H, GLA_DKH) * (GLA_DKH ** -0.5)
    k = k.reshape(B, T, GLA_H, GLA_DKH)
    v = v.reshape(B, T, GLA_H, GLA_DVH)
    z = (jnp.einsum('btr,rk->btk', a_lr, al_w) + al_b).astype(jnp.float32)
    log_a = (jax.nn.log_sigmoid(z) / GLA_TAU).reshape(B, T, GLA_H, GLA_DKH)
    outs = []
    s = gla_s0
    start = 0
    for length, chunk in segments:
        sl = slice(start, start + length)
        o_seg, s = _gla_chunks(q[:, sl], k[:, sl], v[:, sl], log_a[:, sl], s, chunk)
        outs.append(o_seg)
        start += length
    o = jnp.concatenate(outs, axis=1)
    o = o * lax.rsqrt(jnp.mean(o * o, -1, keepdims=True) + RMS_EPS)
    o = o.reshape(B, T, GLA_DV) * gn_g
    y_gla = o.astype(h.dtype) * jax.nn.silu(g_gla)

    mix = jnp.einsum('btc,cd->btd', jnp.concatenate([y_lru, y_gla], axis=-1), w_out)
    h = _layer_norm(DEEPNORM_ALPHA * h + mix, ln1_g, ln1_b)
    ffn = jnp.einsum('btf,fd->btd', jax.nn.silu(jnp.einsum('btd,df->btf', h, wg)) * jnp.einsum('btd,df->btf', h, wu), wd)
    h = _layer_norm(DEEPNORM_ALPHA * h + ffn, ln2_g, ln2_b)
    return h, s.astype(gla_s0.dtype), lru_hT, conv_new


def setup_inputs(seed: int = 0) -> dict:
    key = jax.random.key(seed)
    ks = jax.random.split(key, 32)
    f32 = jnp.float32

    def nrm(k, shape, scale):
        return jax.random.normal(k, shape, f32) * scale

    a0 = jax.random.uniform(ks[15], (DEPTH, LRU_W), f32, 0.9, 0.999)
    s = a0 ** (1.0 / LRU_C)
    lru_lambda = jnp.log(s) - jnp.log1p(-s)
    return {
        "x_prompt": nrm(ks[0], (BATCH, SEQ, D_MODEL), 1.0),
        "x_sample": nrm(ks[1], (DEC_BATCH, DEC_SEQ, D_MODEL), 1.0),
        "state_gla": nrm(ks[2], (DEPTH, DEC_BATCH, GLA_H, GLA_DKH, GLA_DVH), 0.5),
        "state_lru": nrm(ks[3], (DEPTH, DEC_BATCH, LRU_W), 0.5),
        "state_conv": nrm(ks[4], (DEPTH, DEC_BATCH, CONV_K - 1, LRU_W), 1.0),
        "meta_tokens": nrm(ks[5], (N_META, D_MODEL), 1.0),
        "ln_in_g": 1.0 + nrm(ks[6], (D_MODEL,), 0.05),
        "ln_in_b": nrm(ks[7], (D_MODEL,), 0.01),
        "w_in": nrm(ks[8], (DEPTH, D_MODEL, IN_COLS), D_MODEL ** -0.5),
        "conv_w": nrm(ks[9], (DEPTH, CONV_K, LRU_W), CONV_K ** -0.5),
        "conv_b": nrm(ks[10], (DEPTH, LRU_W), 0.01),
        "lru_gate_a_w": nrm(ks[11], (DEPTH, LRU_BLOCKS, LRU_BLK, LRU_BLK), LRU_BLK ** -0.5),
        "lru_gate_a_b": nrm(ks[12], (DEPTH, LRU_W), 0.01),
        "lru_gate_x_w": nrm(ks[13], (DEPTH, LRU_BLOCKS, LRU_BLK, LRU_BLK), LRU_BLK ** -0.5),
        "lru_gate_x_b": nrm(ks[14], (DEPTH, LRU_W), 0.01),
        "lru_lambda": lru_lambda,
        "gla_alpha_w": nrm(ks[16], (DEPTH, GLA_RANK, GLA_DK), GLA_RANK ** -0.5),
        "gla_alpha_b": nrm(ks[17], (DEPTH, GLA_DK), 0.1),
        "gla_norm_g": 1.0 + nrm(ks[18], (DEPTH, GLA_DV), 0.05),
        "w_out": nrm(ks[19], (DEPTH, MIX_W, D_MODEL), MIX_W ** -0.5 * DEEPNORM_BETA),
        "ln1_g": 1.0 + nrm(ks[20], (DEPTH, D_MODEL), 0.05),
        "ln1_b": nrm(ks[21], (DEPTH, D_MODEL), 0.01),
        "w_ffn_gate": nrm(ks[22], (DEPTH, D_MODEL, D_FF), D_MODEL ** -0.5),
        "w_ffn_up": nrm(ks[23], (DEPTH, D_MODEL, D_FF), D_MODEL ** -0.5),
        "w_ffn_down": nrm(ks[24], (DEPTH, D_FF, D_MODEL), D_FF ** -0.5 * DEEPNORM_BETA),
        "ln2_g": 1.0 + nrm(ks[25], (DEPTH, D_MODEL), 0.05),
        "ln2_b": nrm(ks[26], (DEPTH, D_MODEL), 0.01),
    }


def reference(x_prompt, x_sample, state_gla, state_lru, state_conv, meta_tokens, ln_in_g, ln_in_b,
              w_in, conv_w, conv_b, lru_gate_a_w, lru_gate_a_b, lru_gate_x_w, lru_gate_x_b, lru_lambda,
              gla_alpha_w, gla_alpha_b, gla_norm_g, w_out, ln1_g, ln1_b, w_ffn_gate, w_ffn_up, w_ffn_down,
              ln2_g, ln2_b):
    B, T_p, _ = x_prompt.shape
    T_s = x_sample.shape[1]
    meta = jnp.broadcast_to(meta_tokens[None].astype(x_prompt.dtype), (B, N_META, D_MODEL))
    hp = _layer_norm(jnp.concatenate([meta, x_prompt], axis=1), ln_in_g, ln_in_b)
    hs = _layer_norm(x_sample, ln_in_g, ln_in_b)
    seg_p = ((N_META, N_META), (T_p, GLA_CHUNK))
    seg_s = ((T_s, T_s),)
    gla0 = jnp.zeros((B, GLA_H, GLA_DKH, GLA_DVH), hp.dtype)
    lru0 = jnp.zeros((B, LRU_W), hp.dtype)
    conv0 = jnp.zeros((B, CONV_K - 1, LRU_W), hp.dtype)
    gp, lp, cp, gs, ls, cs = [], [], [], [], [], []
    for l in range(DEPTH):
        p = (w_in[l], conv_w[l], conv_b[l], lru_gate_a_w[l], lru_gate_a_b[l], lru_gate_x_w[l],
             lru_gate_x_b[l], lru_lambda[l], gla_alpha_w[l], gla_alpha_b[l], gla_norm_g[l], w_out[l],
             ln1_g[l], ln1_b[l], w_ffn_gate[l], w_ffn_up[l], w_ffn_down[l], ln2_g[l], ln2_b[l])
        hp, g_new, l_new, c_new = _layer(hp, conv0, lru0, gla0, seg_p, p)
        gp.append(g_new)
        lp.append(l_new)
        cp.append(c_new)
        hs, g_new, l_new, c_new = _layer(hs, state_conv[l], state_lru[l], state_gla[l], seg_s, p)
        gs.append(g_new)
        ls.append(l_new)
        cs.append(c_new)
    y_prompt = hp[:, N_META:]
    return (y_prompt, hs, jnp.stack(gp), jnp.stack(lp), jnp.stack(cp), jnp.stack(gs), jnp.stack(ls), jnp.stack(cs))
```

```python
import functools
from typing import NamedTuple

import jax
import jax.numpy as jnp
from jax import lax
from jax.experimental import pallas as pl
from jax.experimental.pallas import tpu as pltpu

f32 = jnp.float32
bf16 = jnp.bfloat16

D_MODEL = 1024
N_META = 16
LRU_W = 512
LRU_BLOCKS = 8
LRU_BLK = LRU_W // LRU_BLOCKS
LRU_C = 8.0
CONV_K = 4
GLA_H = 4
GLA_DKH = 64
GLA_DVH = 128
GLA_DK = GLA_H * GLA_DKH
GLA_DV = GLA_H * GLA_DVH
GLA_RANK = 16
GLA_TAU = 16.0
D_FF = 2816
DEEPNORM_ALPHA = 2.0 ** 0.25
LN_EPS = 1e-5
RMS_EPS = 1e-6

COL_XLRU = 0
COL_GLRU = COL_XLRU + LRU_W
COL_Q = COL_GLRU + LRU_W
COL_K = COL_Q + GLA_DK
COL_V = COL_K + GLA_DK
COL_GGLA = COL_V + GLA_DV
COL_ALR = COL_GGLA + GLA_DV
IN_COLS = COL_ALR + GLA_RANK

LANES = 128
SUBLANES = 8
MXU_DIM = 256
ALR_PAD = LANES
IN_COLS_PAD = COL_ALR + ALR_PAD
VMEM_LIMIT_BYTES = 56 * 1024 * 1024

PROMPT_TILE = 256
GLA_CHUNK = 64
FFN_TILE = 512
SAMPLE_BLOCK = SUBLANES


class MixerParams(NamedTuple):
    ln_g: jax.Array
    ln_b: jax.Array
    w_in: jax.Array
    conv_w: jax.Array
    conv_b: jax.Array
    w_gate: jax.Array
    gate_a_b: jax.Array
    gate_x_b: jax.Array
    lam: jax.Array
    al_w: jax.Array
    al_b: jax.Array
    gn_g: jax.Array
    w_out: jax.Array
    ln1_g: jax.Array
    ln1_b: jax.Array


N_MIXER_PARAMS = len(MixerParams._fields)


def _layer_norm(x, g, b):
    mu = jnp.mean(x, -1, keepdims=True)
    xc = x - mu
    var = jnp.mean(xc * xc, -1, keepdims=True)
    return xc * lax.rsqrt(var + LN_EPS) * g + b


def _dot(a, b):
    return jnp.dot(a, b, preferred_element_type=f32)


def _dot_nt(a, b):
    return lax.dot_general(a, b, (((1,), (1,)), ((), ())), preferred_element_type=f32)


def _dot_tn(a, b):
    return lax.dot_general(a, b, (((0,), (0,)), ((), ())), preferred_element_type=f32)


def _lru_gates(u_half, half, p):
    lo = half * MXU_DIM
    hi = lo + MXU_DIM
    gates = _dot(u_half.astype(bf16), p.w_gate[half])
    r = jax.nn.sigmoid(gates[:, :MXU_DIM] + p.gate_a_b[:, lo:hi])
    i = jax.nn.sigmoid(gates[:, MXU_DIM:] + p.gate_x_b[:, lo:hi])
    log_a = (-LRU_C) * r * jax.nn.softplus(-p.lam[:, lo:hi])
    a = jnp.exp(log_a)
    mult = jnp.sqrt(-jnp.tanh(log_a) * (a * a + 1.0))
    return a, mult * (i * u_half)


def _scan_groups(a, x):
    rows, width = a.shape
    groups = rows // SUBLANES
    a3 = a.reshape(groups, SUBLANES, width)
    x3 = x.reshape(groups, SUBLANES, width)
    sub = lax.broadcasted_iota(jnp.int32, (groups, SUBLANES, width), 1)
    shift = 1
    while shift < SUBLANES:
        keep = sub >= shift
        a_prev = jnp.where(keep, pltpu.roll(a3, shift, 1), 1.0)
        x_prev = jnp.where(keep, pltpu.roll(x3, shift, 1), 0.0)
        x3 = x3 + a3 * x_prev
        a3 = a3 * a_prev
        shift *= 2
    return a3.reshape(rows, width), x3.reshape(rows, width)


def _lru_scan(a, x, h_in):
    rows = a.shape[0]
    big_a, big_x = _scan_groups(a, x)
    carry = h_in
    out = []
    for g in range(rows // SUBLANES):
        sl = slice(g * SUBLANES, (g + 1) * SUBLANES)
        hg = big_a[sl] * carry + big_x[sl]
        carry = hg[SUBLANES - 1:SUBLANES]
        out.append(hg)
    return jnp.concatenate(out, axis=0), carry


def _chunk_cumsum(g, chunk):
    row = lax.broadcasted_iota(jnp.int32, g.shape, 0) % chunk
    shift = 1
    while shift < chunk:
        g = g + jnp.where(row >= shift, pltpu.roll(g, shift, 0), 0.0)
        shift *= 2
    return g


def _head_stack(x, head_masks):
    zero = jnp.zeros_like(x)
    return jnp.concatenate([jnp.where(m, x, zero) for m in head_masks], axis=0)


def _gla_rows(q, k, v, z, chunk, scat_ref):
    rows = q.shape[0]
    n_chunks = rows // chunk
    g = jax.nn.log_sigmoid(z) * (1.0 / GLA_TAU)
    b = _chunk_cumsum(g, chunk)
    last_rows = [b[(c + 1) * chunk - 1:(c + 1) * chunk] for c in range(n_chunks)]
    b_last = jnp.concatenate([jnp.broadcast_to(r, (chunk, GLA_DK)) for r in last_rows], axis=0)
    qs = q * (GLA_DKH ** -0.5)
    q_state = (qs * jnp.exp(b)).astype(bf16)
    k_end = (k * jnp.exp(b_last - b)).astype(bf16)
    q_end = (qs * jnp.exp(b - b_last)).astype(bf16)
    vb = v.astype(bf16)
    pad_rows = SUBLANES - n_chunks
    chunk_decay = jnp.exp(jnp.concatenate(last_rows + [jnp.zeros((pad_rows, GLA_DK), f32)], axis=0)).T

    lane = lax.broadcasted_iota(jnp.int32, (chunk, GLA_DK), 1)
    head_masks = [(lane // GLA_DKH) == h for h in range(GLA_H)]
    t_idx = lax.broadcasted_iota(jnp.int32, (GLA_H * chunk, chunk), 0) % chunk
    s_idx = lax.broadcasted_iota(jnp.int32, (GLA_H * chunk, chunk), 1)
    causal = s_idx <= t_idx

    outs = []
    for c in range(n_chunks):
        sl = slice(c * chunk, (c + 1) * chunk)
        scores = _dot_nt(_head_stack(q_end[sl], head_masks), k_end[sl])
        probs = jnp.where(causal, scores, 0.0).astype(bf16)
        state = scat_ref[...]
        o_state = _dot(_head_stack(q_state[sl], head_masks), state.astype(bf16))
        heads = []
        for h in range(GLA_H):
            hs = slice(h * chunk, (h + 1) * chunk)
            v_h = vb[sl, h * GLA_DVH:(h + 1) * GLA_DVH]
            heads.append(o_state[hs] + _dot(probs[hs], v_h))
        outs.append(jnp.concatenate(heads, axis=1))
        v_stack = jnp.concatenate([vb[sl, h * GLA_DVH:(h + 1) * GLA_DVH] for h in range(GLA_H)], axis=0)
        d_state = _dot_tn(_head_stack(k_end[sl], head_masks), v_stack)
        scat_ref[...] = chunk_decay[:, c:c + 1] * state + d_state
    return jnp.concatenate(outs, axis=0)


def _gla_finish(o, g_gla, gn_g):
    heads = []
    for h in range(GLA_H):
        oh = o[:, h * GLA_DVH:(h + 1) * GLA_DVH]
        heads.append(oh * lax.rsqrt(jnp.mean(oh * oh, -1, keepdims=True) + RMS_EPS))
    return jnp.concatenate(heads, axis=1) * gn_g * jax.nn.silu(g_gla)


def _mixer_rows(x, chunk, p, xl_ref, hcar_ref, scat_ref, want_out):
    rows = x.shape[0]
    h = _layer_norm(x, p.ln_g[...], p.ln_b[...])
    proj = _dot(h.astype(bf16), p.w_in[...])
    x_lru = proj[:, COL_XLRU:COL_GLRU]

    xl_ref[SUBLANES:SUBLANES + rows, :] = x_lru
    cw = p.conv_w[...]
    u = p.conv_b[...] + cw[3:4] * x_lru
    for j in range(CONV_K - 1):
        u = u + cw[j:j + 1] * xl_ref[SUBLANES - (CONV_K - 1) + j:SUBLANES - (CONV_K - 1) + j + rows, :]
    xl_ref[0:SUBLANES, :] = xl_ref[rows:rows + SUBLANES, :]

    h_in = hcar_ref[...]
    hs_halves = []
    carry_halves = []
    for half in range(LRU_W // MXU_DIM):
        lo, hi = half * MXU_DIM, (half + 1) * MXU_DIM
        a, xin = _lru_gates(u[:, lo:hi], half, p)
        hs, carry = _lru_scan(a, xin, h_in[:, lo:hi])
        hs_halves.append(hs)
        carry_halves.append(carry)
    hcar_ref[...] = jnp.concatenate(carry_halves, axis=1)

    z = _dot(proj[:, COL_ALR:COL_ALR + ALR_PAD].astype(bf16), p.al_w[...]) + p.al_b[...]
    o = _gla_rows(proj[:, COL_Q:COL_K], proj[:, COL_K:COL_V], proj[:, COL_V:COL_GGLA], z, chunk, scat_ref)
    if not want_out:
        return None
    y_lru = jnp.concatenate(hs_halves, axis=1) * jax.nn.gelu(proj[:, COL_GLRU:COL_Q])
    y_gla = _gla_finish(o, proj[:, COL_GGLA:COL_ALR], p.gn_g[...])
    return h, jnp.concatenate([y_lru, y_gla], axis=1).astype(bf16)


def _meta_kernel(meta_ref, *refs):
    p = MixerParams(*refs[:N_MIXER_PARAMS])
    xlc_ref, h_ref, s_ref, xl_scr = refs[N_MIXER_PARAMS:]
    xl_scr[...] = jnp.zeros_like(xl_scr)
    h_ref[...] = jnp.zeros_like(h_ref)
    s_ref[...] = jnp.zeros_like(s_ref)
    _mixer_rows(meta_ref[...], N_META, p, xl_scr, h_ref, s_ref, want_out=False)
    xlc_ref[...] = xl_scr[0:SUBLANES, :]


def _prompt_kernel(x_ref, xlc0_ref, h0_ref, s0_ref, *refs):
    p = MixerParams(*refs[:N_MIXER_PARAMS])
    h1_ref, gla_ref, lru_ref, conv_ref, xl_scr, hcar_scr, scat_scr = refs[N_MIXER_PARAMS:]
    t = pl.program_id(1)

    @pl.when(t == 0)
    def _():
        xl_scr[0:SUBLANES, :] = xlc0_ref[...]
        hcar_scr[...] = h0_ref[...]
        scat_scr[...] = s0_ref[...]

    h, y = _mixer_rows(x_ref[0], GLA_CHUNK, p, xl_scr, hcar_scr, scat_scr, want_out=True)
    mix = _dot(y, p.w_out[...])
    h1_ref[0] = _layer_norm(DEEPNORM_ALPHA * h + mix, p.ln1_g[...], p.ln1_b[...])

    @pl.when(t == pl.num_programs(1) - 1)
    def _():
        gla_ref[0] = scat_scr[...]
        lru_ref[0] = hcar_scr[...]
        conv_ref[0] = xl_scr[SUBLANES - (CONV_K - 1):SUBLANES, :]


def _sample_kernel(x_ref, conv_in_ref, lru_in_ref, s_in_ref, *refs):
    p = MixerParams(*refs[:N_MIXER_PARAMS])
    (h1_ref, s_out_ref, lru_out_ref, conv_out_ref,
     hln_scr, ylru_scr, q_scr, k_scr, eg_scr, v_scr, gg_scr, o_scr) = refs[N_MIXER_PARAMS:]
    i = pl.program_id(0)

    @pl.when(i == 0)
    def _():
        h = _layer_norm(x_ref[...], p.ln_g[...], p.ln_b[...])
        hln_scr[...] = h
        proj = _dot(h.astype(bf16), p.w_in[...])
        x_lru = proj[:, COL_XLRU:COL_GLRU]
        cw = p.conv_w[...]
        u = p.conv_b[...] + cw[3:4] * x_lru
        for j in range(CONV_K - 1):
            u = u + cw[j:j + 1] * conv_in_ref[:, j, :]
        for j in range(CONV_K - 2):
            conv_out_ref[:, j, :] = conv_in_ref[:, j + 1, :]
        conv_out_ref[:, CONV_K - 2, :] = x_lru
        h_in = lru_in_ref[...]
        halves = []
        for half in range(LRU_W // MXU_DIM):
            lo, hi = half * MXU_DIM, (half + 1) * MXU_DIM
            a, xin = _lru_gates(u[:, lo:hi], half, p)
            halves.append(a * h_in[:, lo:hi] + xin)
        h_new = jnp.concatenate(halves, axis=1)
        lru_out_ref[...] = h_new
        ylru_scr[...] = h_new * jax.nn.gelu(proj[:, COL_GLRU:COL_Q])
        z = _dot(proj[:, COL_ALR:COL_ALR + ALR_PAD].astype(bf16), p.al_w[...]) + p.al_b[...]
        eg_scr[...] = jnp.exp(jax.nn.log_sigmoid(z) * (1.0 / GLA_TAU))
        q_scr[...] = proj[:, COL_Q:COL_K] * (GLA_DKH ** -0.5)
        k_scr[...] = proj[:, COL_K:COL_V]
        v_scr[...] = proj[:, COL_V:COL_GGLA]
        gg_scr[...] = proj[:, COL_GGLA:COL_ALR]

    r0 = pl.multiple_of(i * SAMPLE_BLOCK, SAMPLE_BLOCK)
    q_t = q_scr[pl.ds(r0, SAMPLE_BLOCK), :].T
    k_t = k_scr[pl.ds(r0, SAMPLE_BLOCK), :].T
    e_t = eg_scr[pl.ds(r0, SAMPLE_BLOCK), :].T
    v_blk = v_scr[pl.ds(r0, SAMPLE_BLOCK), :]
    o_rows = []
    for j in range(SAMPLE_BLOCK):
        heads = []
        for h in range(GLA_H):
            ks = slice(h * GLA_DKH, (h + 1) * GLA_DKH)
            s_new = (e_t[ks, j:j + 1] * s_in_ref[j, h]
                     + k_t[ks, j:j + 1] * v_blk[j:j + 1, h * GLA_DVH:(h + 1) * GLA_DVH])
            s_out_ref[j, h] = s_new
            heads.append(jnp.sum(q_t[ks, j:j + 1] * s_new, axis=0, keepdims=True))
        o_rows.append(jnp.concatenate(heads, axis=1))
    o_scr[pl.ds(r0, SAMPLE_BLOCK), :] = jnp.concatenate(o_rows, axis=0)

    @pl.when(i == pl.num_programs(0) - 1)
    def _():
        y_gla = _gla_finish(o_scr[...], gg_scr[...], p.gn_g[...])
        y = jnp.concatenate([ylru_scr[...], y_gla], axis=1).astype(bf16)
        mix = _dot(y, p.w_out[...])
        h1_ref[...] = _layer_norm(DEEPNORM_ALPHA * hln_scr[...] + mix, p.ln1_g[...], p.ln1_b[...])


def _ffn_kernel(h_ref, wgu_ref, wd_ref, g_ref, b_ref, out_ref):
    h = h_ref[...]
    gu = _dot(h.astype(bf16), wgu_ref[...])
    act = (jax.nn.silu(gu[:, :D_FF]) * gu[:, D_FF:]).astype(bf16)
    ffn = _dot(act, wd_ref[...])
    out_ref[...] = _layer_norm(DEEPNORM_ALPHA * h + ffn, g_ref[...], b_ref[...])


def _block_diag(blocks):
    n, r, c = blocks.shape
    eye = jnp.eye(n, dtype=blocks.dtype)
    return (eye[:, None, :, None] * blocks[:, :, None, :]).reshape(n * r, n * c)


def _const_spec(arr):
    zeros = (0,) * arr.ndim
    return pl.BlockSpec(arr.shape, lambda *_: zeros, pipeline_mode=pl.Buffered(1))


def _mixer_params(ln_in_g, ln_in_b, w_in, conv_w, conv_b, ga_w, ga_b, gx_w, gx_b, lam, al_w, al_b, gn_g, w_out,
                  ln1_g, ln1_b):
    row = lambda a: a.reshape(1, -1).astype(f32)
    w_in_p = jnp.pad(w_in, ((0, 0), (0, IN_COLS_PAD - IN_COLS))).astype(bf16)
    per_half = MXU_DIM // LRU_BLK
    w_gate = jnp.stack([
        jnp.concatenate([_block_diag(ga_w[c * per_half:(c + 1) * per_half]),
                         _block_diag(gx_w[c * per_half:(c + 1) * per_half])], axis=1)
        for c in range(LRU_W // MXU_DIM)]).astype(bf16)
    al_w_p = jnp.pad(al_w, ((0, ALR_PAD - GLA_RANK), (0, 0))).astype(bf16)
    return MixerParams(row(ln_in_g), row(ln_in_b), w_in_p, conv_w.astype(f32), row(conv_b), w_gate, row(ga_b),
                       row(gx_b), row(lam), al_w_p, row(al_b), row(gn_g), w_out.astype(bf16), row(ln1_g), row(ln1_b))


def _meta_call(meta_tokens, params):
    out_shape = (jax.ShapeDtypeStruct((SUBLANES, LRU_W), f32),
                 jax.ShapeDtypeStruct((1, LRU_W), f32),
                 jax.ShapeDtypeStruct((GLA_DK, GLA_DVH), f32))
    return pl.pallas_call(
        _meta_kernel, out_shape=out_shape, name="meta_state",
        scratch_shapes=[pltpu.VMEM((N_META + SUBLANES, LRU_W), f32)],
        compiler_params=pltpu.CompilerParams(vmem_limit_bytes=VMEM_LIMIT_BYTES),
    )(meta_tokens, *params)


def _prompt_call(x, xlc0, h0, s0, params):
    batch, seq, _ = x.shape
    assert seq % PROMPT_TILE == 0 and PROMPT_TILE % GLA_CHUNK == 0
    n_tiles = seq // PROMPT_TILE
    seq_map = lambda b, t: (b, t, 0)
    state_map = lambda b, t: (b, 0, 0)
    out_shape = (jax.ShapeDtypeStruct((batch, seq, D_MODEL), f32),
                 jax.ShapeDtypeStruct((batch, GLA_DK, GLA_DVH), f32),
                 jax.ShapeDtypeStruct((batch, 1, LRU_W), f32),
                 jax.ShapeDtypeStruct((batch, CONV_K - 1, LRU_W), f32))
    return pl.pallas_call(
        _prompt_kernel, out_shape=out_shape, name="prompt_mixer",
        grid=(batch, n_tiles),
        in_specs=[pl.BlockSpec((1, PROMPT_TILE, D_MODEL), seq_map),
                  _const_spec(xlc0), _const_spec(h0), _const_spec(s0)] + [_const_spec(a) for a in params],
        out_specs=(pl.BlockSpec((1, PROMPT_TILE, D_MODEL), seq_map),
                   pl.BlockSpec((1, GLA_DK, GLA_DVH), state_map),
                   pl.BlockSpec((1, 1, LRU_W), state_map),
                   pl.BlockSpec((1, CONV_K - 1, LRU_W), state_map)),
        scratch_shapes=[pltpu.VMEM((PROMPT_TILE + SUBLANES, LRU_W), f32),
                        pltpu.VMEM((1, LRU_W), f32),
                        pltpu.VMEM((GLA_DK, GLA_DVH), f32)],
        compiler_params=pltpu.CompilerParams(dimension_semantics=("arbitrary", "arbitrary"),
                                             vmem_limit_bytes=VMEM_LIMIT_BYTES),
    )(x, xlc0, h0, s0, *params)


def _sample_call(x, conv_in, lru_in, gla_in, params):
    n = x.shape[0]
    assert n % SAMPLE_BLOCK == 0
    state_block = (SAMPLE_BLOCK, GLA_H, GLA_DKH, GLA_DVH)
    state_map = lambda i: (i, 0, 0, 0)
    out_shape = (jax.ShapeDtypeStruct((n, D_MODEL), f32),
                 jax.ShapeDtypeStruct(gla_in.shape, f32),
                 jax.ShapeDtypeStruct((n, LRU_W), f32),
                 jax.ShapeDtypeStruct((n, CONV_K - 1, LRU_W), f32))
    full = lambda shape: pl.BlockSpec(shape, lambda i: (0,) * len(shape))
    return pl.pallas_call(
        _sample_kernel, out_shape=out_shape, name="sample_mixer",
        grid=(n // SAMPLE_BLOCK,),
        in_specs=[_const_spec(x), _const_spec(conv_in), _const_spec(lru_in),
                  pl.BlockSpec(state_block, state_map)] + [_const_spec(a) for a in params],
        out_specs=(full((n, D_MODEL)), pl.BlockSpec(state_block, state_map), full((n, LRU_W)),
                   full((n, CONV_K - 1, LRU_W))),
        scratch_shapes=[pltpu.VMEM((n, D_MODEL), f32), pltpu.VMEM((n, LRU_W), f32),
                        pltpu.VMEM((n, GLA_DK), f32), pltpu.VMEM((n, GLA_DK), f32), pltpu.VMEM((n, GLA_DK), f32),
                        pltpu.VMEM((n, GLA_DV), f32), pltpu.VMEM((n, GLA_DV), f32), pltpu.VMEM((n, GLA_DV), f32)],
        compiler_params=pltpu.CompilerParams(dimension_semantics=("arbitrary",),
                                             vmem_limit_bytes=VMEM_LIMIT_BYTES),
    )(x, conv_in, lru_in, gla_in, *params)


def _ffn_call(h, w_gu, w_d, ln_g, ln_b, tile):
    n = h.shape[0]
    assert n % tile == 0
    row_map = lambda i: (i, 0)
    return pl.pallas_call(
        _ffn_kernel, out_shape=jax.ShapeDtypeStruct((n, D_MODEL), f32), name="ffn",
        grid=(n // tile,),
        in_specs=[pl.BlockSpec((tile, D_MODEL), row_map), _const_spec(w_gu), _const_spec(w_d),
                  _const_spec(ln_g), _const_spec(ln_b)],
        out_specs=pl.BlockSpec((tile, D_MODEL), row_map),
        compiler_params=pltpu.CompilerParams(dimension_semantics=("arbitrary",),
                                             vmem_limit_bytes=VMEM_LIMIT_BYTES),
    )(h, w_gu, w_d, ln_g, ln_b)


def kernel(x_prompt, x_sample, state_gla, state_lru, state_conv, meta_tokens, ln_in_g, ln_in_b, w_in, conv_w, conv_b, lru_gate_a_w, lru_gate_a_b, lru_gate_x_w, lru_gate_x_b, lru_lambda, gla_alpha_w, gla_alpha_b, gla_norm_g, w_out, ln1_g, ln1_b, w_ffn_gate, w_ffn_up, w_ffn_down, ln2_g, ln2_b):
    assert w_in.shape[0] == 1 and x_sample.shape[1] == 1, "one layer, one decode token per sample sequence"
    batch, seq, _ = x_prompt.shape
    n_sample = x_sample.shape[0]
    params = _mixer_params(ln_in_g, ln_in_b, w_in[0], conv_w[0], conv_b[0], lru_gate_a_w[0], lru_gate_a_b[0],
                           lru_gate_x_w[0], lru_gate_x_b[0], lru_lambda[0], gla_alpha_w[0], gla_alpha_b[0],
                           gla_norm_g[0], w_out[0], ln1_g[0], ln1_b[0])
    w_gu = jnp.concatenate([w_ffn_gate[0], w_ffn_up[0]], axis=1).astype(bf16)
    w_d = w_ffn_down[0].astype(bf16)
    ln2g = ln2_g[0].reshape(1, -1)
    ln2b = ln2_b[0].reshape(1, -1)

    xlc0, h0, s0 = _meta_call(meta_tokens, params)
    h1_p, gla_p, lru_p, conv_p = _prompt_call(x_prompt, xlc0, h0, s0, params)
    h1_s, gla_s, lru_s, conv_s = _sample_call(x_sample[:, 0, :], state_conv[0], state_lru[0], state_gla[0], params)

    y_p = _ffn_call(h1_p.reshape(batch * seq, D_MODEL), w_gu, w_d, ln2g, ln2b, FFN_TILE)
    y_s = _ffn_call(h1_s, w_gu, w_d, ln2g, ln2b, n_sample)
    return (y_p.reshape(batch, seq, D_MODEL),
            y_s.reshape(n_sample, 1, D_MODEL),
            gla_p.reshape(1, batch, GLA_H, GLA_DKH, GLA_DVH),
            lru_p.reshape(1, batch, LRU_W),
            conv_p[None],
            gla_s[None],
            lru_s[None],
            conv_s[None])
```

```python
import types
from typing import NamedTuple

import jax
import jax.numpy as jnp
from jax import lax
from jax.experimental import pallas as pl
from jax.experimental.pallas import tpu as pltpu

f32 = jnp.float32
bf16 = jnp.bfloat16

D_MODEL = 1024
N_META = 16
LRU_W = 512
LRU_BLOCKS = 8
LRU_BLK = LRU_W // LRU_BLOCKS
LRU_C = 8.0
CONV_K = 4
GLA_H = 4
GLA_DKH = 64
GLA_DVH = 128
GLA_DK = GLA_H * GLA_DKH
GLA_DV = GLA_H * GLA_DVH
GLA_RANK = 16
GLA_TAU = 16.0
D_FF = 2816
DEEPNORM_ALPHA = 2.0 ** 0.25
LN_EPS = 1e-5
RMS_EPS = 1e-6

COL_XLRU = 0
COL_GLRU = COL_XLRU + LRU_W
COL_Q = COL_GLRU + LRU_W
COL_K = COL_Q + GLA_DK
COL_V = COL_K + GLA_DK
COL_GGLA = COL_V + GLA_DV
COL_ALR = COL_GGLA + GLA_DV
IN_COLS = COL_ALR + GLA_RANK

LANES = 128
SUBLANES = 8
MXU_DIM = 256
ALR_PAD = LANES
IN_COLS_PAD = COL_ALR + ALR_PAD
VMEM_LIMIT_BYTES = 56 * 1024 * 1024

PROMPT_TILE = 512
PROMPT_SUBTILE = 256
GLA_CHUNK = 64
FFN_TILE = 512
SAMPLE_BLOCK = SUBLANES


class MixerParams(NamedTuple):
    ln_g: jax.Array
    ln_b: jax.Array
    w_in: jax.Array
    conv_w: jax.Array
    conv_b: jax.Array
    w_gate: jax.Array
    gate_a_b: jax.Array
    gate_x_b: jax.Array
    lam: jax.Array
    al_w: jax.Array
    al_b: jax.Array
    gn_g: jax.Array
    w_out: jax.Array
    ln1_g: jax.Array
    ln1_b: jax.Array


N_MIXER_PARAMS = len(MixerParams._fields)


def _layer_norm(x, g, b):
    mu = jnp.mean(x, -1, keepdims=True)
    xc = x - mu
    var = jnp.mean(xc * xc, -1, keepdims=True)
    return xc * lax.rsqrt(var + LN_EPS) * g + b


def _dot(a, b):
    return jnp.dot(a, b, preferred_element_type=f32)


def _dot_nt(a, b):
    return lax.dot_general(a, b, (((1,), (1,)), ((), ())), preferred_element_type=f32)


def _dot_tn(a, b):
    return lax.dot_general(a, b, (((0,), (0,)), ((), ())), preferred_element_type=f32)


def _lru_gates(u_half, half, p):
    lo = half * MXU_DIM
    hi = lo + MXU_DIM
    gates = _dot(u_half.astype(bf16), p.w_gate[half])
    r = jax.nn.sigmoid(gates[:, :MXU_DIM] + p.gate_a_b[:, lo:hi])
    i = jax.nn.sigmoid(gates[:, MXU_DIM:] + p.gate_x_b[:, lo:hi])
    log_a = (-LRU_C) * r * jax.nn.softplus(-p.lam[:, lo:hi])
    a = jnp.exp(log_a)
    mult = jnp.sqrt(-jnp.tanh(log_a) * (a * a + 1.0))
    return a, mult * (i * u_half)


def _scan_groups(a, x):
    rows, width = a.shape
    groups = rows // SUBLANES
    a3 = a.reshape(groups, SUBLANES, width)
    x3 = x.reshape(groups, SUBLANES, width)
    sub = lax.broadcasted_iota(jnp.int32, (groups, SUBLANES, width), 1)
    shift = 1
    while shift < SUBLANES:
        keep = sub >= shift
        a_prev = jnp.where(keep, pltpu.roll(a3, shift, 1), 1.0)
        x_prev = jnp.where(keep, pltpu.roll(x3, shift, 1), 0.0)
        x3 = x3 + a3 * x_prev
        a3 = a3 * a_prev
        shift *= 2
    return a3.reshape(rows, width), x3.reshape(rows, width)


def _lru_scan(a, x, h_in):
    rows = a.shape[0]
    big_a, big_x = _scan_groups(a, x)
    carry = h_in
    out = []
    for g in range(rows // SUBLANES):
        sl = slice(g * SUBLANES, (g + 1) * SUBLANES)
        hg = big_a[sl] * carry + big_x[sl]
        carry = hg[SUBLANES - 1:SUBLANES]
        out.append(hg)
    return jnp.concatenate(out, axis=0), carry


def _chunk_cumsum(g, chunk):
    row = lax.broadcasted_iota(jnp.int32, g.shape, 0) % chunk
    shift = 1
    while shift < chunk:
        g = g + jnp.where(row >= shift, pltpu.roll(g, shift, 0), 0.0)
        shift *= 2
    return g


def _chunk_cumsum_mxu(g, tri):
    hi = g.astype(bf16)
    lo = (g - hi.astype(f32)).astype(bf16)
    return _dot(tri, hi) + _dot(tri, lo)


def _head_stack(x, head_masks):
    zero = jnp.zeros_like(x)
    return jnp.concatenate([jnp.where(m, x, zero) for m in head_masks], axis=0)


def _gla_finish(o, g_gla, gn_g):
    heads = []
    for h in range(GLA_H):
        oh = o[:, h * GLA_DVH:(h + 1) * GLA_DVH]
        heads.append(oh * lax.rsqrt(jnp.mean(oh * oh, -1, keepdims=True) + RMS_EPS))
    return jnp.concatenate(heads, axis=1) * gn_g * jax.nn.silu(g_gla)


def _phase_project(t, x_of, p):
    def norm():
        t.h = _layer_norm(x_of(), p.ln_g[...], p.ln_b[...])
        t.hb = t.h.astype(bf16)

    def lru():
        t.p_lru = _dot(t.hb, p.w_in[:, COL_XLRU:COL_Q])

    def qk():
        t.p_qk = _dot(t.hb, p.w_in[:, COL_Q:COL_V])

    def v():
        t.p_v = _dot(t.hb, p.w_in[:, COL_V:COL_GGLA])

    def g():
        t.p_g = _dot(t.hb, p.w_in[:, COL_GGLA:IN_COLS_PAD])

    return [norm, lru, qk, v, g]


def _phase_lru(t, p, xl_ref, hcar_ref, want_out=True):
    halves = range(LRU_W // MXU_DIM)
    t.hs, t.carry, t.gates = {}, {}, {}

    def conv():
        x_lru = t.p_lru[:, :LRU_W]
        rows = x_lru.shape[0]
        xl_ref[SUBLANES:SUBLANES + rows, :] = x_lru
        cw = p.conv_w[...]
        u = p.conv_b[...] + cw[CONV_K - 1:CONV_K] * x_lru
        for j in range(CONV_K - 1):
            start = SUBLANES - (CONV_K - 1) + j
            u = u + cw[j:j + 1] * xl_ref[start:start + rows, :]
        xl_ref[0:SUBLANES, :] = xl_ref[rows:rows + SUBLANES, :]
        t.u = u
        t.h_in = hcar_ref[...]

    def gates(half):
        def step():
            t.gates[half] = _lru_gates(t.u[:, half * MXU_DIM:(half + 1) * MXU_DIM], half, p)
        return step

    def scan(half):
        def step():
            a, xin = t.gates[half]
            t.hs[half], t.carry[half] = _lru_scan(a, xin, t.h_in[:, half * MXU_DIM:(half + 1) * MXU_DIM])
        return step

    def finish():
        hcar_ref[...] = jnp.concatenate([t.carry[h] for h in halves], axis=1)
        if want_out:
            hs = jnp.concatenate([t.hs[h] for h in halves], axis=1)
            t.y_lru = (hs * jax.nn.gelu(t.p_lru[:, LRU_W:])).astype(bf16)

    steps = [conv]
    for half in halves:
        steps += [gates(half), scan(half)]
    return steps + [finish]


def _phase_gla(t, p, chunk, scat_ref, tri=None, want_out=True):
    outs = []

    def prepare():
        q, k = t.p_qk[:, :GLA_DK], t.p_qk[:, GLA_DK:]
        n_chunks = q.shape[0] // chunk
        z = _dot(t.p_g[:, GLA_DV:].astype(bf16), p.al_w[...]) + p.al_b[...]
        g = jax.nn.log_sigmoid(z) * (1.0 / GLA_TAU)
        b = _chunk_cumsum(g, chunk) if tri is None else _chunk_cumsum_mxu(g, tri)
        last_rows = [b[(c + 1) * chunk - 1:(c + 1) * chunk] for c in range(n_chunks)]
        b_last = jnp.concatenate([jnp.broadcast_to(r, (chunk, GLA_DK)) for r in last_rows], axis=0)
        qs = q * (GLA_DKH ** -0.5)
        t.q_state = (qs * jnp.exp(b)).astype(bf16)
        t.k_end = (k * jnp.exp(b_last - b)).astype(bf16)
        t.q_end = (qs * jnp.exp(b - b_last)).astype(bf16)
        t.vb = t.p_v.astype(bf16)
        pad = jnp.zeros((SUBLANES - n_chunks, GLA_DK), f32)
        t.chunk_decay = jnp.exp(jnp.concatenate(last_rows + [pad], axis=0)).T
        lane = lax.broadcasted_iota(jnp.int32, (chunk, GLA_DK), 1)
        t.head_masks = [(lane // GLA_DKH) == h for h in range(GLA_H)]
        t_idx = lax.broadcasted_iota(jnp.int32, (GLA_H * chunk, chunk), 0) % chunk
        s_idx = lax.broadcasted_iota(jnp.int32, (GLA_H * chunk, chunk), 1)
        t.causal = s_idx <= t_idx

    def one_chunk(c):
        def step():
            sl = slice(c * chunk, (c + 1) * chunk)
            scores = _dot_nt(_head_stack(t.q_end[sl], t.head_masks), t.k_end[sl])
            probs = jnp.where(t.causal, scores, 0.0).astype(bf16)
            state = scat_ref[...]
            o_state = _dot(_head_stack(t.q_state[sl], t.head_masks), state.astype(bf16))
            v_heads = [t.vb[sl, h * GLA_DVH:(h + 1) * GLA_DVH] for h in range(GLA_H)]
            outs.append(jnp.concatenate(
                [o_state[h * chunk:(h + 1) * chunk] + _dot(probs[h * chunk:(h + 1) * chunk], v_heads[h])
                 for h in range(GLA_H)], axis=1))
            d_state = _dot_tn(_head_stack(t.k_end[sl], t.head_masks), jnp.concatenate(v_heads, axis=0))
            scat_ref[...] = t.chunk_decay[:, c:c + 1] * state + d_state
        return step

    def finish():
        if want_out:
            t.y_gla = _gla_finish(jnp.concatenate(outs, axis=0), t.p_g[:, :GLA_DV], p.gn_g[...]).astype(bf16)

    return [prepare] + [one_chunk(c) for c in range(t.rows // chunk)] + [finish]


def _phase_out(t, p, store):
    def project():
        t.mix = _dot(t.y_lru, p.w_out[0:LRU_W, :]) + _dot(t.y_gla, p.w_out[LRU_W:, :])

    def norm():
        store(_layer_norm(DEEPNORM_ALPHA * t.h + t.mix, p.ln1_g[...], p.ln1_b[...]))

    return [project, norm]


def _run_interleaved(*phases):
    keyed = []
    for n, steps in enumerate(phases):
        keyed += [((i + 0.5) / len(steps), n, step) for i, step in enumerate(steps)]
    for _, _, step in sorted(keyed, key=lambda e: e[:2]):
        step()


def _meta_kernel(meta_ref, *refs):
    p = MixerParams(*refs[:N_MIXER_PARAMS])
    xlc_ref, h_ref, s_ref, xl_scr = refs[N_MIXER_PARAMS:]
    xl_scr[...] = jnp.zeros_like(xl_scr)
    h_ref[...] = jnp.zeros_like(h_ref)
    s_ref[...] = jnp.zeros_like(s_ref)
    t = types.SimpleNamespace(rows=N_META)
    _run_interleaved(_phase_project(t, lambda: meta_ref[...], p))
    _run_interleaved(_phase_lru(t, p, xl_scr, h_ref, want_out=False))
    _run_interleaved(_phase_gla(t, p, N_META, s_ref, want_out=False))
    xlc_ref[...] = xl_scr[0:SUBLANES, :]


def _prompt_kernel(x_ref, xlc0_ref, h0_ref, s0_ref, tri_ref, *refs):
    p = MixerParams(*refs[:N_MIXER_PARAMS])
    h1_ref, gla_ref, lru_ref, conv_ref, xl_scr, hcar_scr, scat_scr = refs[N_MIXER_PARAMS:]
    step = pl.program_id(1)

    @pl.when(step == 0)
    def _():
        xl_scr[0:SUBLANES, :] = xlc0_ref[...]
        hcar_scr[...] = h0_ref[...]
        scat_scr[...] = s0_ref[...]

    n_sub = PROMPT_TILE // PROMPT_SUBTILE
    tiles = [types.SimpleNamespace(rows=PROMPT_SUBTILE) for _ in range(n_sub)]
    tri = tri_ref[...]

    def rows(s):
        return slice(s * PROMPT_SUBTILE, (s + 1) * PROMPT_SUBTILE)

    def project(s):
        return _phase_project(tiles[s], lambda: x_ref[0, rows(s), :], p)

    def out(s):
        def store(v):
            h1_ref[0, rows(s), :] = v
        return _phase_out(tiles[s], p, store)

    _run_interleaved(project(0))
    for s in range(n_sub):
        mixers = (_phase_lru(tiles[s], p, xl_scr, hcar_scr)
                  + _phase_gla(tiles[s], p, GLA_CHUNK, scat_scr, tri))
        others = ([project(s + 1)] if s + 1 < n_sub else []) + ([out(s - 1)] if s >= 1 else [])
        _run_interleaved(mixers, *others)
    _run_interleaved(out(n_sub - 1))

    @pl.when(step == pl.num_programs(1) - 1)
    def _():
        gla_ref[0] = scat_scr[...]
        lru_ref[0] = hcar_scr[...]
        conv_ref[0] = xl_scr[SUBLANES - (CONV_K - 1):SUBLANES, :]


def _sample_kernel(x_ref, conv_in_ref, lru_in_ref, s_in_ref, *refs):
    p = MixerParams(*refs[:N_MIXER_PARAMS])
    (h1_ref, s_out_ref, lru_out_ref, conv_out_ref,
     hln_scr, ylru_scr, q_scr, k_scr, eg_scr, v_scr, gg_scr, o_scr) = refs[N_MIXER_PARAMS:]
    i = pl.program_id(0)

    @pl.when(i == 0)
    def _():
        h = _layer_norm(x_ref[...], p.ln_g[...], p.ln_b[...])
        hln_scr[...] = h
        proj = _dot(h.astype(bf16), p.w_in[...])
        x_lru = proj[:, COL_XLRU:COL_GLRU]
        cw = p.conv_w[...]
        u = p.conv_b[...] + cw[3:4] * x_lru
        for j in range(CONV_K - 1):
            u = u + cw[j:j + 1] * conv_in_ref[:, j, :]
        for j in range(CONV_K - 2):
            conv_out_ref[:, j, :] = conv_in_ref[:, j + 1, :]
        conv_out_ref[:, CONV_K - 2, :] = x_lru
        h_in = lru_in_ref[...]
        halves = []
        for half in range(LRU_W // MXU_DIM):
            lo, hi = half * MXU_DIM, (half + 1) * MXU_DIM
            a, xin = _lru_gates(u[:, lo:hi], half, p)
            halves.append(a * h_in[:, lo:hi] + xin)
        h_new = jnp.concatenate(halves, axis=1)
        lru_out_ref[...] = h_new
        ylru_scr[...] = h_new * jax.nn.gelu(proj[:, COL_GLRU:COL_Q])
        z = _dot(proj[:, COL_ALR:COL_ALR + ALR_PAD].astype(bf16), p.al_w[...]) + p.al_b[...]
        eg_scr[...] = jnp.exp(jax.nn.log_sigmoid(z) * (1.0 / GLA_TAU))
        q_scr[...] = proj[:, COL_Q:COL_K] * (GLA_DKH ** -0.5)
        k_scr[...] = proj[:, COL_K:COL_V]
        v_scr[...] = proj[:, COL_V:COL_GGLA]
        gg_scr[...] = proj[:, COL_GGLA:COL_ALR]

    r0 = pl.multiple_of(i * SAMPLE_BLOCK, SAMPLE_BLOCK)
    q_t = q_scr[pl.ds(r0, SAMPLE_BLOCK), :].T
    k_t = k_scr[pl.ds(r0, SAMPLE_BLOCK), :].T
    e_t = eg_scr[pl.ds(r0, SAMPLE_BLOCK), :].T
    v_blk = v_scr[pl.ds(r0, SAMPLE_BLOCK), :]
    o_rows = []
    for j in range(SAMPLE_BLOCK):
        heads = []
        for h in range(GLA_H):
            ks = slice(h * GLA_DKH, (h + 1) * GLA_DKH)
            s_new = (e_t[ks, j:j + 1] * s_in_ref[j, h]
                     + k_t[ks, j:j + 1] * v_blk[j:j + 1, h * GLA_DVH:(h + 1) * GLA_DVH])
            s_out_ref[j, h] = s_new
            heads.append(jnp.sum(q_t[ks, j:j + 1] * s_new, axis=0, keepdims=True))
        o_rows.append(jnp.concatenate(heads, axis=1))
    o_scr[pl.ds(r0, SAMPLE_BLOCK), :] = jnp.concatenate(o_rows, axis=0)

    @pl.when(i == pl.num_programs(0) - 1)
    def _():
        y_gla = _gla_finish(o_scr[...], gg_scr[...], p.gn_g[...])
        y = jnp.concatenate([ylru_scr[...], y_gla], axis=1).astype(bf16)
        mix = _dot(y, p.w_out[...])
        h1_ref[...] = _layer_norm(DEEPNORM_ALPHA * hln_scr[...] + mix, p.ln1_g[...], p.ln1_b[...])


def _ffn_kernel(h_ref, wgu_ref, wd_ref, g_ref, b_ref, out_ref):
    h = h_ref[...]
    gu = _dot(h.astype(bf16), wgu_ref[...])
    act = (jax.nn.silu(gu[:, :D_FF]) * gu[:, D_FF:]).astype(bf16)
    ffn = _dot(act, wd_ref[...])
    out_ref[...] = _layer_norm(DEEPNORM_ALPHA * h + ffn, g_ref[...], b_ref[...])


def _block_diag(blocks):
    n, r, c = blocks.shape
    eye = jnp.eye(n, dtype=blocks.dtype)
    return (eye[:, None, :, None] * blocks[:, :, None, :]).reshape(n * r, n * c)


def _const_spec(arr):
    zeros = (0,) * arr.ndim
    return pl.BlockSpec(arr.shape, lambda *_: zeros, pipeline_mode=pl.Buffered(1))


def _mixer_params(ln_in_g, ln_in_b, w_in, conv_w, conv_b, ga_w, ga_b, gx_w, gx_b, lam, al_w, al_b, gn_g, w_out,
                  ln1_g, ln1_b):
    row = lambda a: a.reshape(1, -1).astype(f32)
    w_in_p = jnp.pad(w_in, ((0, 0), (0, IN_COLS_PAD - IN_COLS))).astype(bf16)
    per_half = MXU_DIM // LRU_BLK
    w_gate = jnp.stack([
        jnp.concatenate([_block_diag(ga_w[c * per_half:(c + 1) * per_half]),
                         _block_diag(gx_w[c * per_half:(c + 1) * per_half])], axis=1)
        for c in range(LRU_W // MXU_DIM)]).astype(bf16)
    al_w_p = jnp.pad(al_w, ((0, ALR_PAD - GLA_RANK), (0, 0))).astype(bf16)
    return MixerParams(row(ln_in_g), row(ln_in_b), w_in_p, conv_w.astype(f32), row(conv_b), w_gate, row(ga_b),
                       row(gx_b), row(lam), al_w_p, row(al_b), row(gn_g), w_out.astype(bf16), row(ln1_g), row(ln1_b))


def _meta_call(meta_tokens, params):
    out_shape = (jax.ShapeDtypeStruct((SUBLANES, LRU_W), f32),
                 jax.ShapeDtypeStruct((1, LRU_W), f32),
                 jax.ShapeDtypeStruct((GLA_DK, GLA_DVH), f32))
    return pl.pallas_call(
        _meta_kernel, out_shape=out_shape, name="meta_state",
        scratch_shapes=[pltpu.VMEM((N_META + SUBLANES, LRU_W), f32)],
        compiler_params=pltpu.CompilerParams(vmem_limit_bytes=VMEM_LIMIT_BYTES),
    )(meta_tokens, *params)


def _prompt_call(x, xlc0, h0, s0, params):
    batch, seq, _ = x.shape
    assert seq % PROMPT_TILE == 0 and PROMPT_TILE % PROMPT_SUBTILE == 0 and PROMPT_SUBTILE % GLA_CHUNK == 0
    n_tiles = seq // PROMPT_TILE
    pos = jnp.arange(PROMPT_SUBTILE)
    tri = ((pos[:, None] >= pos[None, :])
           & (pos[:, None] // GLA_CHUNK == pos[None, :] // GLA_CHUNK)).astype(bf16)
    seq_map = lambda b, t: (b, t, 0)
    state_map = lambda b, t: (b, 0, 0)
    out_shape = (jax.ShapeDtypeStruct((batch, seq, D_MODEL), f32),
                 jax.ShapeDtypeStruct((batch, GLA_DK, GLA_DVH), f32),
                 jax.ShapeDtypeStruct((batch, 1, LRU_W), f32),
                 jax.ShapeDtypeStruct((batch, CONV_K - 1, LRU_W), f32))
    return pl.pallas_call(
        _prompt_kernel, out_shape=out_shape, name="prompt_mixer",
        grid=(batch, n_tiles),
        in_specs=[pl.BlockSpec((1, PROMPT_TILE, D_MODEL), seq_map),
                  _const_spec(xlc0), _const_spec(h0), _const_spec(s0), _const_spec(tri)]
                 + [_const_spec(a) for a in params],
        out_specs=(pl.BlockSpec((1, PROMPT_TILE, D_MODEL), seq_map),
                   pl.BlockSpec((1, GLA_DK, GLA_DVH), state_map),
                   pl.BlockSpec((1, 1, LRU_W), state_map),
                   pl.BlockSpec((1, CONV_K - 1, LRU_W), state_map)),
        scratch_shapes=[pltpu.VMEM((PROMPT_SUBTILE + SUBLANES, LRU_W), f32),
                        pltpu.VMEM((1, LRU_W), f32),
                        pltpu.VMEM((GLA_DK, GLA_DVH), f32)],
        compiler_params=pltpu.CompilerParams(dimension_semantics=("arbitrary", "arbitrary"),
                                             vmem_limit_bytes=VMEM_LIMIT_BYTES),
    )(x, xlc0, h0, s0, tri, *params)


def _sample_call(x, conv_in, lru_in, gla_in, params):
    n = x.shape[0]
    assert n % SAMPLE_BLOCK == 0
    state_block = (SAMPLE_BLOCK, GLA_H, GLA_DKH, GLA_DVH)
    state_map = lambda i: (i, 0, 0, 0)
    out_shape = (jax.ShapeDtypeStruct((n, D_MODEL), f32),
                 jax.ShapeDtypeStruct(gla_in.shape, f32),
                 jax.ShapeDtypeStruct((n, LRU_W), f32),
                 jax.ShapeDtypeStruct((n, CONV_K - 1, LRU_W), f32))
    full = lambda shape: pl.BlockSpec(shape, lambda i: (0,) * len(shape))
    return pl.pallas_call(
        _sample_kernel, out_shape=out_shape, name="sample_mixer",
        grid=(n // SAMPLE_BLOCK,),
        in_specs=[_const_spec(x), _const_spec(conv_in), _const_spec(lru_in),
                  pl.BlockSpec(state_block, state_map)] + [_const_spec(a) for a in params],
        out_specs=(full((n, D_MODEL)), pl.BlockSpec(state_block, state_map), full((n, LRU_W)),
                   full((n, CONV_K - 1, LRU_W))),
        scratch_shapes=[pltpu.VMEM((n, D_MODEL), f32), pltpu.VMEM((n, LRU_W), f32),
                        pltpu.VMEM((n, GLA_DK), f32), pltpu.VMEM((n, GLA_DK), f32), pltpu.VMEM((n, GLA_DK), f32),
                        pltpu.VMEM((n, GLA_DV), f32), pltpu.VMEM((n, GLA_DV), f32), pltpu.VMEM((n, GLA_DV), f32)],
        compiler_params=pltpu.CompilerParams(dimension_semantics=("arbitrary",),
                                             vmem_limit_bytes=VMEM_LIMIT_BYTES),
    )(x, conv_in, lru_in, gla_in, *params)


def _ffn_call(h, w_gu, w_d, ln_g, ln_b, tile):
    n = h.shape[0]
    assert n % tile == 0
    row_map = lambda i: (i, 0)
    return pl.pallas_call(
        _ffn_kernel, out_shape=jax.ShapeDtypeStruct((n, D_MODEL), f32), name="ffn",
        grid=(n // tile,),
        in_specs=[pl.BlockSpec((tile, D_MODEL), row_map), _const_spec(w_gu), _const_spec(w_d),
                  _const_spec(ln_g), _const_spec(ln_b)],
        out_specs=pl.BlockSpec((tile, D_MODEL), row_map),
        compiler_params=pltpu.CompilerParams(dimension_semantics=("arbitrary",),
                                             vmem_limit_bytes=VMEM_LIMIT_BYTES),
    )(h, w_gu, w_d, ln_g, ln_b)


def kernel(x_prompt, x_sample, state_gla, state_lru, state_conv, meta_tokens, ln_in_g, ln_in_b, w_in, conv_w, conv_b, lru_gate_a_w, lru_gate_a_b, lru_gate_x_w, lru_gate_x_b, lru_lambda, gla_alpha_w, gla_alpha_b, gla_norm_g, w_out, ln1_g, ln1_b, w_ffn_gate, w_ffn_up, w_ffn_down, ln2_g, ln2_b):
    assert w_in.shape[0] == 1 and x_sample.shape[1] == 1, "one layer, one decode token per sample sequence"
    batch, seq, _ = x_prompt.shape
    n_sample = x_sample.shape[0]
    params = _mixer_params(ln_in_g, ln_in_b, w_in[0], conv_w[0], conv_b[0], lru_gate_a_w[0], lru_gate_a_b[0],
                           lru_gate_x_w[0], lru_gate_x_b[0], lru_lambda[0], gla_alpha_w[0], gla_alpha_b[0],
                           gla_norm_g[0], w_out[0], ln1_g[0], ln1_b[0])
    w_gu = jnp.concatenate([w_ffn_gate[0], w_ffn_up[0]], axis=1).astype(bf16)
    w_d = w_ffn_down[0].astype(bf16)
    ln2g = ln2_g[0].reshape(1, -1)
    ln2b = ln2_b[0].reshape(1, -1)

    xlc0, h0, s0 = _meta_call(meta_tokens, params)
    h1_p, gla_p, lru_p, conv_p = _prompt_call(x_prompt, xlc0, h0, s0, params)
    h1_s, gla_s, lru_s, conv_s = _sample_call(x_sample[:, 0, :], state_conv[0], state_lru[0], state_gla[0], params)

    y_p = _ffn_call(h1_p.reshape(batch * seq, D_MODEL), w_gu, w_d, ln2g, ln2b, FFN_TILE)
    y_s = _ffn_call(h1_s, w_gu, w_d, ln2g, ln2b, n_sample)
    return (y_p.reshape(batch, seq, D_MODEL),
            y_s.reshape(n_sample, 1, D_MODEL),
            gla_p.reshape(1, batch, GLA_H, GLA_DKH, GLA_DVH),
            lru_p.reshape(1, batch, LRU_W),
            conv_p[None],
            gla_s[None],
            lru_s[None],
            conv_s[None])
```

```python
import types
from typing import NamedTuple

import jax
import jax.numpy as jnp
from jax import lax
from jax.experimental import pallas as pl
from jax.experimental.pallas import tpu as pltpu

f32 = jnp.float32
bf16 = jnp.bfloat16

D_MODEL = 1024
N_META = 16
LRU_W = 512
LRU_BLOCKS = 8
LRU_BLK = LRU_W // LRU_BLOCKS
LRU_C = 8.0
CONV_K = 4
GLA_H = 4
GLA_DKH = 64
GLA_DVH = 128
GLA_DK = GLA_H * GLA_DKH
GLA_DV = GLA_H * GLA_DVH
GLA_RANK = 16
GLA_TAU = 16.0
D_FF = 2816
DEEPNORM_ALPHA = 2.0 ** 0.25
LN_EPS = 1e-5
RMS_EPS = 1e-6

COL_XLRU = 0
COL_GLRU = COL_XLRU + LRU_W
COL_Q = COL_GLRU + LRU_W
COL_K = COL_Q + GLA_DK
COL_V = COL_K + GLA_DK
COL_GGLA = COL_V + GLA_DV
COL_ALR = COL_GGLA + GLA_DV
IN_COLS = COL_ALR + GLA_RANK

LANES = 128
SUBLANES = 8
MXU_DIM = 256
ALR_PAD = LANES
IN_COLS_PAD = COL_ALR + ALR_PAD
VMEM_LIMIT_BYTES = 56 * 1024 * 1024

PROMPT_TILE = 512
PROMPT_SUBTILE = 256
GLA_CHUNK = 64
SAMPLE_BLOCK = SUBLANES


class MixerParams(NamedTuple):
    ln_g: jax.Array
    ln_b: jax.Array
    w_in: jax.Array
    conv_w: jax.Array
    conv_b: jax.Array
    w_gate: jax.Array
    gate_a_b: jax.Array
    gate_x_b: jax.Array
    lam: jax.Array
    al_w: jax.Array
    al_b: jax.Array
    gn_g: jax.Array
    w_out: jax.Array
    ln1_g: jax.Array
    ln1_b: jax.Array


N_MIXER_PARAMS = len(MixerParams._fields)


class FfnParams(NamedTuple):
    w_gate: jax.Array
    w_up: jax.Array
    w_down: jax.Array
    ln2_g: jax.Array
    ln2_b: jax.Array


N_FFN_PARAMS = len(FfnParams._fields)


def _layer_norm(x, g, b):
    mu = jnp.mean(x, -1, keepdims=True)
    xc = x - mu
    var = jnp.mean(xc * xc, -1, keepdims=True)
    return xc * lax.rsqrt(var + LN_EPS) * g + b


def _dot(a, b):
    return jnp.dot(a, b, preferred_element_type=f32)


def _dot_nt(a, b):
    return lax.dot_general(a, b, (((1,), (1,)), ((), ())), preferred_element_type=f32)


def _dot_tn(a, b):
    return lax.dot_general(a, b, (((0,), (0,)), ((), ())), preferred_element_type=f32)


def _lru_gates(u_half, half, p):
    lo = half * MXU_DIM
    hi = lo + MXU_DIM
    gates = _dot(u_half.astype(bf16), p.w_gate[half])
    r = jax.nn.sigmoid(gates[:, :MXU_DIM] + p.gate_a_b[:, lo:hi])
    i = jax.nn.sigmoid(gates[:, MXU_DIM:] + p.gate_x_b[:, lo:hi])
    log_a = (-LRU_C) * r * jax.nn.softplus(-p.lam[:, lo:hi])
    a = jnp.exp(log_a)
    mult = jnp.sqrt(-jnp.tanh(log_a) * (a * a + 1.0))
    return a, mult * (i * u_half)


def _scan_groups(a, x):
    rows, width = a.shape
    groups = rows // SUBLANES
    a3 = a.reshape(groups, SUBLANES, width)
    x3 = x.reshape(groups, SUBLANES, width)
    sub = lax.broadcasted_iota(jnp.int32, (groups, SUBLANES, width), 1)
    shift = 1
    while shift < SUBLANES:
        keep = sub >= shift
        a_prev = jnp.where(keep, pltpu.roll(a3, shift, 1), 1.0)
        x_prev = jnp.where(keep, pltpu.roll(x3, shift, 1), 0.0)
        x3 = x3 + a3 * x_prev
        a3 = a3 * a_prev
        shift *= 2
    return a3.reshape(rows, width), x3.reshape(rows, width)


def _lru_scan(a, x, h_in):
    rows = a.shape[0]
    big_a, big_x = _scan_groups(a, x)
    carry = h_in
    out = []
    for g in range(rows // SUBLANES):
        sl = slice(g * SUBLANES, (g + 1) * SUBLANES)
        hg = big_a[sl] * carry + big_x[sl]
        carry = hg[SUBLANES - 1:SUBLANES]
        out.append(hg)
    return jnp.concatenate(out, axis=0), carry


def _chunk_cumsum(g, chunk):
    row = lax.broadcasted_iota(jnp.int32, g.shape, 0) % chunk
    shift = 1
    while shift < chunk:
        g = g + jnp.where(row >= shift, pltpu.roll(g, shift, 0), 0.0)
        shift *= 2
    return g


def _chunk_cumsum_mxu(g, tri):
    hi = g.astype(bf16)
    lo = (g - hi.astype(f32)).astype(bf16)
    return _dot(tri, hi) + _dot(tri, lo)


def _head_stack(x, head_masks):
    zero = jnp.zeros_like(x)
    return jnp.concatenate([jnp.where(m, x, zero) for m in head_masks], axis=0)


def _gla_finish(o, g_gla, gn_g):
    heads = []
    for h in range(GLA_H):
        oh = o[:, h * GLA_DVH:(h + 1) * GLA_DVH]
        heads.append(oh * lax.rsqrt(jnp.mean(oh * oh, -1, keepdims=True) + RMS_EPS))
    return jnp.concatenate(heads, axis=1) * gn_g * jax.nn.silu(g_gla)


def _phase_project(t, x_of, p):
    def norm():
        t.h = _layer_norm(x_of(), p.ln_g[...], p.ln_b[...])
        t.hb = t.h.astype(bf16)

    def lru():
        t.p_lru = _dot(t.hb, p.w_in[:, COL_XLRU:COL_Q])

    def qk():
        t.p_qk = _dot(t.hb, p.w_in[:, COL_Q:COL_V])

    def v():
        t.p_v = _dot(t.hb, p.w_in[:, COL_V:COL_GGLA])

    def g():
        t.p_g = _dot(t.hb, p.w_in[:, COL_GGLA:IN_COLS_PAD])

    return [norm, lru, qk, v, g]


def _phase_lru(t, p, xl_ref, hcar_ref, want_out=True):
    halves = range(LRU_W // MXU_DIM)
    t.hs, t.carry, t.gates = {}, {}, {}

    def conv():
        x_lru = t.p_lru[:, :LRU_W]
        rows = x_lru.shape[0]
        xl_ref[SUBLANES:SUBLANES + rows, :] = x_lru
        cw = p.conv_w[...]
        u = p.conv_b[...] + cw[CONV_K - 1:CONV_K] * x_lru
        for j in range(CONV_K - 1):
            start = SUBLANES - (CONV_K - 1) + j
            u = u + cw[j:j + 1] * xl_ref[start:start + rows, :]
        xl_ref[0:SUBLANES, :] = xl_ref[rows:rows + SUBLANES, :]
        t.u = u
        t.h_in = hcar_ref[...]

    def gates(half):
        def step():
            t.gates[half] = _lru_gates(t.u[:, half * MXU_DIM:(half + 1) * MXU_DIM], half, p)
        return step

    def scan(half):
        def step():
            a, xin = t.gates[half]
            t.hs[half], t.carry[half] = _lru_scan(a, xin, t.h_in[:, half * MXU_DIM:(half + 1) * MXU_DIM])
        return step

    def finish():
        hcar_ref[...] = jnp.concatenate([t.carry[h] for h in halves], axis=1)
        if want_out:
            hs = jnp.concatenate([t.hs[h] for h in halves], axis=1)
            t.y_lru = (hs * jax.nn.gelu(t.p_lru[:, LRU_W:])).astype(bf16)

    steps = [conv]
    for half in halves:
        steps += [gates(half), scan(half)]
    return steps + [finish]


def _phase_gla(t, p, chunk, scat_ref, tri=None, want_out=True):
    outs = []

    def prepare():
        q, k = t.p_qk[:, :GLA_DK], t.p_qk[:, GLA_DK:]
        n_chunks = q.shape[0] // chunk
        z = _dot(t.p_g[:, GLA_DV:].astype(bf16), p.al_w[...]) + p.al_b[...]
        g = jax.nn.log_sigmoid(z) * (1.0 / GLA_TAU)
        b = _chunk_cumsum(g, chunk) if tri is None else _chunk_cumsum_mxu(g, tri)
        last_rows = [b[(c + 1) * chunk - 1:(c + 1) * chunk] for c in range(n_chunks)]
        b_last = jnp.concatenate([jnp.broadcast_to(r, (chunk, GLA_DK)) for r in last_rows], axis=0)
        qs = q * (GLA_DKH ** -0.5)
        t.q_state = (qs * jnp.exp(b)).astype(bf16)
        t.k_end = (k * jnp.exp(b_last - b)).astype(bf16)
        t.q_end = (qs * jnp.exp(b - b_last)).astype(bf16)
        t.vb = t.p_v.astype(bf16)
        pad = jnp.zeros((SUBLANES - n_chunks, GLA_DK), f32)
        t.chunk_decay = jnp.exp(jnp.concatenate(last_rows + [pad], axis=0)).T
        lane = lax.broadcasted_iota(jnp.int32, (chunk, GLA_DK), 1)
        t.head_masks = [(lane // GLA_DKH) == h for h in range(GLA_H)]
        t_idx = lax.broadcasted_iota(jnp.int32, (GLA_H * chunk, chunk), 0) % chunk
        s_idx = lax.broadcasted_iota(jnp.int32, (GLA_H * chunk, chunk), 1)
        t.causal = s_idx <= t_idx

    def one_chunk(c):
        def step():
            sl = slice(c * chunk, (c + 1) * chunk)
            scores = _dot_nt(_head_stack(t.q_end[sl], t.head_masks), t.k_end[sl])
            probs = jnp.where(t.causal, scores, 0.0).astype(bf16)
            state = scat_ref[...]
            o_state = _dot(_head_stack(t.q_state[sl], t.head_masks), state.astype(bf16))
            v_heads = [t.vb[sl, h * GLA_DVH:(h + 1) * GLA_DVH] for h in range(GLA_H)]
            outs.append(jnp.concatenate(
                [o_state[h * chunk:(h + 1) * chunk] + _dot(probs[h * chunk:(h + 1) * chunk], v_heads[h])
                 for h in range(GLA_H)], axis=1))
            d_state = _dot_tn(_head_stack(t.k_end[sl], t.head_masks), jnp.concatenate(v_heads, axis=0))
            scat_ref[...] = t.chunk_decay[:, c:c + 1] * state + d_state
        return step

    def finish():
        if want_out:
            t.y_gla = _gla_finish(jnp.concatenate(outs, axis=0), t.p_g[:, :GLA_DV], p.gn_g[...]).astype(bf16)

    return [prepare] + [one_chunk(c) for c in range(t.rows // chunk)] + [finish]


def _phase_out(t, p):
    mix = []

    def project(j):
        def step():
            cols = slice(j * MXU_DIM, (j + 1) * MXU_DIM)
            mix.append(_dot(t.y_lru, p.w_out[0:LRU_W, cols]) + _dot(t.y_gla, p.w_out[LRU_W:, cols]))
        return step

    def norm():
        t.h1 = _layer_norm(DEEPNORM_ALPHA * t.h + jnp.concatenate(mix, axis=1), p.ln1_g[...], p.ln1_b[...])
        t.h1b = t.h1.astype(bf16)

    return [project(j) for j in range(D_MODEL // MXU_DIM)] + [norm]


def _phase_ffn(t, fp, store):
    acts, outs = [], []

    def up(n):
        def step():
            cols = slice(n * MXU_DIM, (n + 1) * MXU_DIM)
            gate = _dot(t.h1b, fp.w_gate[:, cols])
            acts.append((jax.nn.silu(gate) * _dot(t.h1b, fp.w_up[:, cols])).astype(bf16))
        return step

    def down(j):
        def step():
            if j == 0:
                t.act = jnp.concatenate(acts, axis=1)
            outs.append(_dot(t.act, fp.w_down[:, j * MXU_DIM:(j + 1) * MXU_DIM]))
        return step

    def norm():
        ffn = jnp.concatenate(outs, axis=1)
        store(_layer_norm(DEEPNORM_ALPHA * t.h1 + ffn, fp.ln2_g[...], fp.ln2_b[...]))

    return ([up(n) for n in range(D_FF // MXU_DIM)] + [down(j) for j in range(D_MODEL // MXU_DIM)] + [norm])


def _run_interleaved(*phases):
    keyed = []
    for n, steps in enumerate(phases):
        keyed += [((i + 0.5) / len(steps), n, step) for i, step in enumerate(steps)]
    for _, _, step in sorted(keyed, key=lambda e: e[:2]):
        step()


def _meta_kernel(meta_ref, *refs):
    p = MixerParams(*refs[:N_MIXER_PARAMS])
    xlc_ref, h_ref, s_ref, xl_scr = refs[N_MIXER_PARAMS:]
    xl_scr[...] = jnp.zeros_like(xl_scr)
    h_ref[...] = jnp.zeros_like(h_ref)
    s_ref[...] = jnp.zeros_like(s_ref)
    t = types.SimpleNamespace(rows=N_META)
    _run_interleaved(_phase_project(t, lambda: meta_ref[...], p))
    _run_interleaved(_phase_lru(t, p, xl_scr, h_ref, want_out=False))
    _run_interleaved(_phase_gla(t, p, N_META, s_ref, want_out=False))
    xlc_ref[...] = xl_scr[0:SUBLANES, :]


def _prompt_kernel(x_ref, xlc0_ref, h0_ref, s0_ref, tri_ref, *refs):
    p = MixerParams(*refs[:N_MIXER_PARAMS])
    fp = FfnParams(*refs[N_MIXER_PARAMS:N_MIXER_PARAMS + N_FFN_PARAMS])
    y_ref, gla_ref, lru_ref, conv_ref, xl_scr, hcar_scr, scat_scr = refs[N_MIXER_PARAMS + N_FFN_PARAMS:]
    step = pl.program_id(1)

    @pl.when(step == 0)
    def _():
        xl_scr[0:SUBLANES, :] = xlc0_ref[...]
        hcar_scr[...] = h0_ref[...]
        scat_scr[...] = s0_ref[...]

    n_sub = PROMPT_TILE // PROMPT_SUBTILE
    tiles = [types.SimpleNamespace(rows=PROMPT_SUBTILE) for _ in range(n_sub)]
    tri = tri_ref[...]

    def rows(s):
        return slice(s * PROMPT_SUBTILE, (s + 1) * PROMPT_SUBTILE)

    def project(s):
        return _phase_project(tiles[s], lambda: x_ref[0, rows(s), :], p)

    def out(s):
        def store(v):
            y_ref[0, rows(s), :] = v
        return _phase_out(tiles[s], p) + _phase_ffn(tiles[s], fp, store)

    _run_interleaved(project(0))
    for s in range(n_sub):
        mixers = (_phase_lru(tiles[s], p, xl_scr, hcar_scr)
                  + _phase_gla(tiles[s], p, GLA_CHUNK, scat_scr, tri))
        others = ([project(s + 1)] if s + 1 < n_sub else []) + ([out(s - 1)] if s >= 1 else [])
        _run_interleaved(mixers, *others)
    _run_interleaved(out(n_sub - 1))

    @pl.when(step == pl.num_programs(1) - 1)
    def _():
        gla_ref[0] = scat_scr[...]
        lru_ref[0] = hcar_scr[...]
        conv_ref[0] = xl_scr[SUBLANES - (CONV_K - 1):SUBLANES, :]


def _sample_kernel(x_ref, conv_in_ref, lru_in_ref, s_in_ref, *refs):
    p = MixerParams(*refs[:N_MIXER_PARAMS])
    (h1_ref, s_out_ref, lru_out_ref, conv_out_ref,
     hln_scr, ylru_scr, q_scr, k_scr, eg_scr, v_scr, gg_scr, o_scr) = refs[N_MIXER_PARAMS:]
    i = pl.program_id(0)

    @pl.when(i == 0)
    def _():
        h = _layer_norm(x_ref[...], p.ln_g[...], p.ln_b[...])
        hln_scr[...] = h
        proj = _dot(h.astype(bf16), p.w_in[...])
        x_lru = proj[:, COL_XLRU:COL_GLRU]
        cw = p.conv_w[...]
        u = p.conv_b[...] + cw[3:4] * x_lru
        for j in range(CONV_K - 1):
            u = u + cw[j:j + 1] * conv_in_ref[:, j, :]
        for j in range(CONV_K - 2):
            conv_out_ref[:, j, :] = conv_in_ref[:, j + 1, :]
        conv_out_ref[:, CONV_K - 2, :] = x_lru
        h_in = lru_in_ref[...]
        halves = []
        for half in range(LRU_W // MXU_DIM):
            lo, hi = half * MXU_DIM, (half + 1) * MXU_DIM
            a, xin = _lru_gates(u[:, lo:hi], half, p)
            halves.append(a * h_in[:, lo:hi] + xin)
        h_new = jnp.concatenate(halves, axis=1)
        lru_out_ref[...] = h_new
        ylru_scr[...] = h_new * jax.nn.gelu(proj[:, COL_GLRU:COL_Q])
        z = _dot(proj[:, COL_ALR:COL_ALR + ALR_PAD].astype(bf16), p.al_w[...]) + p.al_b[...]
        eg_scr[...] = jnp.exp(jax.nn.log_sigmoid(z) * (1.0 / GLA_TAU))
        q_scr[...] = proj[:, COL_Q:COL_K] * (GLA_DKH ** -0.5)
        k_scr[...] = proj[:, COL_K:COL_V]
        v_scr[...] = proj[:, COL_V:COL_GGLA]
        gg_scr[...] = proj[:, COL_GGLA:COL_ALR]

    r0 = pl.multiple_of(i * SAMPLE_BLOCK, SAMPLE_BLOCK)
    q_t = q_scr[pl.ds(r0, SAMPLE_BLOCK), :].T
    k_t = k_scr[pl.ds(r0, SAMPLE_BLOCK), :].T
    e_t = eg_scr[pl.ds(r0, SAMPLE_BLOCK), :].T
    v_blk = v_scr[pl.ds(r0, SAMPLE_BLOCK), :]
    o_rows = []
    for j in range(SAMPLE_BLOCK):
        heads = []
        for h in range(GLA_H):
            ks = slice(h * GLA_DKH, (h + 1) * GLA_DKH)
            s_new = (e_t[ks, j:j + 1] * s_in_ref[j, h]
                     + k_t[ks, j:j + 1] * v_blk[j:j + 1, h * GLA_DVH:(h + 1) * GLA_DVH])
            s_out_ref[j, h] = s_new
            heads.append(jnp.sum(q_t[ks, j:j + 1] * s_new, axis=0, keepdims=True))
        o_rows.append(jnp.concatenate(heads, axis=1))
    o_scr[pl.ds(r0, SAMPLE_BLOCK), :] = jnp.concatenate(o_rows, axis=0)

    @pl.when(i == pl.num_programs(0) - 1)
    def _():
        y_gla = _gla_finish(o_scr[...], gg_scr[...], p.gn_g[...])
        y = jnp.concatenate([ylru_scr[...], y_gla], axis=1).astype(bf16)
        mix = _dot(y, p.w_out[...])
        h1_ref[...] = _layer_norm(DEEPNORM_ALPHA * hln_scr[...] + mix, p.ln1_g[...], p.ln1_b[...])


def _ffn_kernel(h_ref, *refs):
    fp = FfnParams(*refs[:N_FFN_PARAMS])
    out_ref, = refs[N_FFN_PARAMS:]
    h1 = h_ref[...]
    t = types.SimpleNamespace(h1=h1, h1b=h1.astype(bf16))

    def store(v):
        out_ref[...] = v

    _run_interleaved(_phase_ffn(t, fp, store))


def _block_diag(blocks):
    n, r, c = blocks.shape
    eye = jnp.eye(n, dtype=blocks.dtype)
    return (eye[:, None, :, None] * blocks[:, :, None, :]).reshape(n * r, n * c)


def _const_spec(arr):
    zeros = (0,) * arr.ndim
    return pl.BlockSpec(arr.shape, lambda *_: zeros, pipeline_mode=pl.Buffered(1))


def _mixer_params(ln_in_g, ln_in_b, w_in, conv_w, conv_b, ga_w, ga_b, gx_w, gx_b, lam, al_w, al_b, gn_g, w_out,
                  ln1_g, ln1_b):
    row = lambda a: a.reshape(1, -1).astype(f32)
    w_in_p = jnp.pad(w_in, ((0, 0), (0, IN_COLS_PAD - IN_COLS))).astype(bf16)
    per_half = MXU_DIM // LRU_BLK
    w_gate = jnp.stack([
        jnp.concatenate([_block_diag(ga_w[c * per_half:(c + 1) * per_half]),
                         _block_diag(gx_w[c * per_half:(c + 1) * per_half])], axis=1)
        for c in range(LRU_W // MXU_DIM)]).astype(bf16)
    al_w_p = jnp.pad(al_w, ((0, ALR_PAD - GLA_RANK), (0, 0))).astype(bf16)
    return MixerParams(row(ln_in_g), row(ln_in_b), w_in_p, conv_w.astype(f32), row(conv_b), w_gate, row(ga_b),
                       row(gx_b), row(lam), al_w_p, row(al_b), row(gn_g), w_out.astype(bf16), row(ln1_g), row(ln1_b))


def _meta_call(meta_tokens, params):
    out_shape = (jax.ShapeDtypeStruct((SUBLANES, LRU_W), f32),
                 jax.ShapeDtypeStruct((1, LRU_W), f32),
                 jax.ShapeDtypeStruct((GLA_DK, GLA_DVH), f32))
    return pl.pallas_call(
        _meta_kernel, out_shape=out_shape, name="meta_state",
        scratch_shapes=[pltpu.VMEM((N_META + SUBLANES, LRU_W), f32)],
        compiler_params=pltpu.CompilerParams(vmem_limit_bytes=VMEM_LIMIT_BYTES),
    )(meta_tokens, *params)


def _prompt_call(x, xlc0, h0, s0, params, ffn_params):
    batch, seq, _ = x.shape
    assert seq % PROMPT_TILE == 0 and PROMPT_TILE % PROMPT_SUBTILE == 0 and PROMPT_SUBTILE % GLA_CHUNK == 0
    n_tiles = seq // PROMPT_TILE
    pos = jnp.arange(PROMPT_SUBTILE)
    tri = ((pos[:, None] >= pos[None, :])
           & (pos[:, None] // GLA_CHUNK == pos[None, :] // GLA_CHUNK)).astype(bf16)
    seq_map = lambda b, t: (b, t, 0)
    state_map = lambda b, t: (b, 0, 0)
    out_shape = (jax.ShapeDtypeStruct((batch, seq, D_MODEL), f32),
                 jax.ShapeDtypeStruct((batch, GLA_DK, GLA_DVH), f32),
                 jax.ShapeDtypeStruct((batch, 1, LRU_W), f32),
                 jax.ShapeDtypeStruct((batch, CONV_K - 1, LRU_W), f32))
    return pl.pallas_call(
        _prompt_kernel, out_shape=out_shape, name="prompt_mixer",
        grid=(batch, n_tiles),
        in_specs=[pl.BlockSpec((1, PROMPT_TILE, D_MODEL), seq_map),
                  _const_spec(xlc0), _const_spec(h0), _const_spec(s0), _const_spec(tri)]
                 + [_const_spec(a) for a in params] + [_const_spec(a) for a in ffn_params],
        out_specs=(pl.BlockSpec((1, PROMPT_TILE, D_MODEL), seq_map),
                   pl.BlockSpec((1, GLA_DK, GLA_DVH), state_map),
                   pl.BlockSpec((1, 1, LRU_W), state_map),
                   pl.BlockSpec((1, CONV_K - 1, LRU_W), state_map)),
        scratch_shapes=[pltpu.VMEM((PROMPT_SUBTILE + SUBLANES, LRU_W), f32),
                        pltpu.VMEM((1, LRU_W), f32),
                        pltpu.VMEM((GLA_DK, GLA_DVH), f32)],
        compiler_params=pltpu.CompilerParams(dimension_semantics=("arbitrary", "arbitrary"),
                                             vmem_limit_bytes=VMEM_LIMIT_BYTES),
    )(x, xlc0, h0, s0, tri, *params, *ffn_params)


def _sample_call(x, conv_in, lru_in, gla_in, params):
    n = x.shape[0]
    assert n % SAMPLE_BLOCK == 0
    state_block = (SAMPLE_BLOCK, GLA_H, GLA_DKH, GLA_DVH)
    state_map = lambda i: (i, 0, 0, 0)
    out_shape = (jax.ShapeDtypeStruct((n, D_MODEL), f32),
                 jax.ShapeDtypeStruct(gla_in.shape, f32),
                 jax.ShapeDtypeStruct((n, LRU_W), f32),
                 jax.ShapeDtypeStruct((n, CONV_K - 1, LRU_W), f32))
    full = lambda shape: pl.BlockSpec(shape, lambda i: (0,) * len(shape))
    return pl.pallas_call(
        _sample_kernel, out_shape=out_shape, name="sample_mixer",
        grid=(n // SAMPLE_BLOCK,),
        in_specs=[_const_spec(x), _const_spec(conv_in), _const_spec(lru_in),
                  pl.BlockSpec(state_block, state_map)] + [_const_spec(a) for a in params],
        out_specs=(full((n, D_MODEL)), pl.BlockSpec(state_block, state_map), full((n, LRU_W)),
                   full((n, CONV_K - 1, LRU_W))),
        scratch_shapes=[pltpu.VMEM((n, D_MODEL), f32), pltpu.VMEM((n, LRU_W), f32),
                        pltpu.VMEM((n, GLA_DK), f32), pltpu.VMEM((n, GLA_DK), f32), pltpu.VMEM((n, GLA_DK), f32),
                        pltpu.VMEM((n, GLA_DV), f32), pltpu.VMEM((n, GLA_DV), f32), pltpu.VMEM((n, GLA_DV), f32)],
        compiler_params=pltpu.CompilerParams(dimension_semantics=("arbitrary",),
                                             vmem_limit_bytes=VMEM_LIMIT_BYTES),
    )(x, conv_in, lru_in, gla_in, *params)


def _ffn_call(h, ffn_params):
    return pl.pallas_call(
        _ffn_kernel, out_shape=jax.ShapeDtypeStruct(h.shape, f32), name="sample_ffn",
        compiler_params=pltpu.CompilerParams(vmem_limit_bytes=VMEM_LIMIT_BYTES),
    )(h, *ffn_params)


def kernel(x_prompt, x_sample, state_gla, state_lru, state_conv, meta_tokens, ln_in_g, ln_in_b, w_in, conv_w, conv_b, lru_gate_a_w, lru_gate_a_b, lru_gate_x_w, lru_gate_x_b, lru_lambda, gla_alpha_w, gla_alpha_b, gla_norm_g, w_out, ln1_g, ln1_b, w_ffn_gate, w_ffn_up, w_ffn_down, ln2_g, ln2_b):
    assert w_in.shape[0] == 1 and x_sample.shape[1] == 1, "one layer, one decode token per sample sequence"
    batch, seq, _ = x_prompt.shape
    n_sample = x_sample.shape[0]
    params = _mixer_params(ln_in_g, ln_in_b, w_in[0], conv_w[0], conv_b[0], lru_gate_a_w[0], lru_gate_a_b[0],
                           lru_gate_x_w[0], lru_gate_x_b[0], lru_lambda[0], gla_alpha_w[0], gla_alpha_b[0],
                           gla_norm_g[0], w_out[0], ln1_g[0], ln1_b[0])
    ffn_params = FfnParams(w_ffn_gate[0].astype(bf16), w_ffn_up[0].astype(bf16), w_ffn_down[0].astype(bf16),
                           ln2_g[0].reshape(1, -1), ln2_b[0].reshape(1, -1))

    xlc0, h0, s0 = _meta_call(meta_tokens, params)
    y_p, gla_p, lru_p, conv_p = _prompt_call(x_prompt, xlc0, h0, s0, params, ffn_params)
    h1_s, gla_s, lru_s, conv_s = _sample_call(x_sample[:, 0, :], state_conv[0], state_lru[0], state_gla[0], params)
    y_s = _ffn_call(h1_s, ffn_params)
    return (y_p,
            y_s.reshape(n_sample, 1, D_MODEL),
            gla_p.reshape(1, batch, GLA_H, GLA_DKH, GLA_DVH),
            lru_p.reshape(1, batch, LRU_W),
            conv_p[None],
            gla_s[None],
            lru_s[None],
            conv_s[None])
```

```python
import functools
import types
from typing import NamedTuple

import jax
import jax.numpy as jnp
from jax import lax
from jax.experimental import pallas as pl
from jax.experimental.pallas import tpu as pltpu

f32 = jnp.float32
bf16 = jnp.bfloat16

D_MODEL = 1024
N_META = 16
LRU_W = 512
LRU_BLOCKS = 8
LRU_BLK = LRU_W // LRU_BLOCKS
LRU_C = 8.0
CONV_K = 4
GLA_H = 4
GLA_DKH = 64
GLA_DVH = 128
GLA_DK = GLA_H * GLA_DKH
GLA_DV = GLA_H * GLA_DVH
GLA_RANK = 16
GLA_TAU = 16.0
D_FF = 2816
DEEPNORM_ALPHA = 2.0 ** 0.25
LN_EPS = 1e-5
RMS_EPS = 1e-6

COL_XLRU = 0
COL_GLRU = COL_XLRU + LRU_W
COL_Q = COL_GLRU + LRU_W
COL_K = COL_Q + GLA_DK
COL_V = COL_K + GLA_DK
COL_GGLA = COL_V + GLA_DV
COL_ALR = COL_GGLA + GLA_DV
IN_COLS = COL_ALR + GLA_RANK

LANES = 128
SUBLANES = 8
MXU_DIM = 256
ALR_PAD = LANES
IN_COLS_PAD = COL_ALR + ALR_PAD
VMEM_LIMIT_BYTES = 56 * 1024 * 1024

PROMPT_TILE = 512
PROMPT_SUBTILE = 256
GLA_CHUNK = 64
SAMPLE_BLOCK = SUBLANES


class MixerParams(NamedTuple):
    ln_g: jax.Array
    ln_b: jax.Array
    w_in: jax.Array
    conv_w: jax.Array
    conv_b: jax.Array
    w_gate: jax.Array
    gate_a_b: jax.Array
    gate_x_b: jax.Array
    lam: jax.Array
    al_w: jax.Array
    al_b: jax.Array
    gn_g: jax.Array
    w_out: jax.Array
    ln1_g: jax.Array
    ln1_b: jax.Array


N_MIXER_PARAMS = len(MixerParams._fields)


class FfnParams(NamedTuple):
    w_gate: jax.Array
    w_up: jax.Array
    w_down: jax.Array
    ln2_g: jax.Array
    ln2_b: jax.Array


N_FFN_PARAMS = len(FfnParams._fields)


def _layer_norm(x, g, b):
    mu = jnp.mean(x, -1, keepdims=True)
    xc = x - mu
    var = jnp.mean(xc * xc, -1, keepdims=True)
    return xc * lax.rsqrt(var + LN_EPS) * g + b


def _dot(a, b):
    return jnp.dot(a, b, preferred_element_type=f32)


def _dot_nt(a, b):
    return lax.dot_general(a, b, (((1,), (1,)), ((), ())), preferred_element_type=f32)


def _dot_tn(a, b):
    return lax.dot_general(a, b, (((0,), (0,)), ((), ())), preferred_element_type=f32)


def _lru_gates(u_half, half, p):
    lo = half * MXU_DIM
    hi = lo + MXU_DIM
    gates = _dot(u_half.astype(bf16), p.w_gate[half])
    r = jax.nn.sigmoid(gates[:, :MXU_DIM] + p.gate_a_b[:, lo:hi])
    i = jax.nn.sigmoid(gates[:, MXU_DIM:] + p.gate_x_b[:, lo:hi])
    log_a = (-LRU_C) * r * jax.nn.softplus(-p.lam[:, lo:hi])
    a = jnp.exp(log_a)
    mult = jnp.sqrt(-jnp.tanh(log_a) * (a * a + 1.0))
    return a, mult * (i * u_half)


def _scan_groups(a, x):
    rows, width = a.shape
    groups = rows // SUBLANES
    a3 = a.reshape(groups, SUBLANES, width)
    x3 = x.reshape(groups, SUBLANES, width)
    sub = lax.broadcasted_iota(jnp.int32, (groups, SUBLANES, width), 1)
    shift = 1
    while shift < SUBLANES:
        keep = sub >= shift
        a_prev = jnp.where(keep, pltpu.roll(a3, shift, 1), 1.0)
        x_prev = jnp.where(keep, pltpu.roll(x3, shift, 1), 0.0)
        x3 = x3 + a3 * x_prev
        a3 = a3 * a_prev
        shift *= 2
    return a3.reshape(rows, width), x3.reshape(rows, width)


def _lru_scan(a, x, h_in):
    rows = a.shape[0]
    big_a, big_x = _scan_groups(a, x)
    carry = h_in
    out = []
    for g in range(rows // SUBLANES):
        sl = slice(g * SUBLANES, (g + 1) * SUBLANES)
        hg = big_a[sl] * carry + big_x[sl]
        carry = hg[SUBLANES - 1:SUBLANES]
        out.append(hg)
    return jnp.concatenate(out, axis=0), carry


def _chunk_cumsum(g, chunk):
    row = lax.broadcasted_iota(jnp.int32, g.shape, 0) % chunk
    shift = 1
    while shift < chunk:
        g = g + jnp.where(row >= shift, pltpu.roll(g, shift, 0), 0.0)
        shift *= 2
    return g


def _chunk_cumsum_mxu(g, tri):
    hi = g.astype(bf16)
    lo = (g - hi.astype(f32)).astype(bf16)
    return _dot(tri, hi) + _dot(tri, lo)


def _head_stack(x, head_masks):
    zero = jnp.zeros_like(x)
    return jnp.concatenate([jnp.where(m, x, zero) for m in head_masks], axis=0)


def _gla_finish(o, g_gla, gn_g):
    heads = []
    for h in range(GLA_H):
        oh = o[:, h * GLA_DVH:(h + 1) * GLA_DVH]
        heads.append(oh * lax.rsqrt(jnp.mean(oh * oh, -1, keepdims=True) + RMS_EPS))
    return jnp.concatenate(heads, axis=1) * gn_g * jax.nn.silu(g_gla)


STEP_COST = dict(
    in_norm=0.6, in_lru=1.0, in_qk=0.5, in_v=0.5, in_g=0.6,
    conv=0.9, gates=0.7, scan=0.8, lru_finish=0.5,
    gla_prepare=0.9, gla_chunk=0.4, gla_finish=0.5,
    out_cols=0.25, out_norm=0.5,
    ffn_up=0.5, ffn_down=0.7, ffn_norm=0.5,
    save=0.05,
)


def _phase_project(t, x_of, p):
    def norm():
        t.h = _layer_norm(x_of(), p.ln_g[...], p.ln_b[...])
        t.hb = t.h.astype(bf16)

    def lru():
        t.p_lru = _dot(t.hb, p.w_in[:, COL_XLRU:COL_Q])

    def qk():
        t.p_qk = _dot(t.hb, p.w_in[:, COL_Q:COL_V])

    def v():
        t.p_v = _dot(t.hb, p.w_in[:, COL_V:COL_GGLA])

    def g():
        t.p_g = _dot(t.hb, p.w_in[:, COL_GGLA:IN_COLS_PAD])

    c = STEP_COST
    return [(norm, c["in_norm"]), (lru, c["in_lru"]), (qk, c["in_qk"]), (v, c["in_v"]), (g, c["in_g"])]


def _phase_lru(t, p, xl_ref, hcar_ref, want_out=True):
    halves = range(LRU_W // MXU_DIM)
    t.hs, t.carry, t.gates = {}, {}, {}

    def conv():
        x_lru = t.p_lru[:, :LRU_W]
        rows = x_lru.shape[0]
        xl_ref[SUBLANES:SUBLANES + rows, :] = x_lru
        cw = p.conv_w[...]
        u = p.conv_b[...] + cw[CONV_K - 1:CONV_K] * x_lru
        for j in range(CONV_K - 1):
            start = SUBLANES - (CONV_K - 1) + j
            u = u + cw[j:j + 1] * xl_ref[start:start + rows, :]
        xl_ref[0:SUBLANES, :] = xl_ref[rows:rows + SUBLANES, :]
        t.u = u
        t.h_in = hcar_ref[...]

    def gates(half):
        def step():
            t.gates[half] = _lru_gates(t.u[:, half * MXU_DIM:(half + 1) * MXU_DIM], half, p)
        return step

    def scan(half):
        def step():
            a, xin = t.gates[half]
            t.hs[half], t.carry[half] = _lru_scan(a, xin, t.h_in[:, half * MXU_DIM:(half + 1) * MXU_DIM])
        return step

    def finish():
        hcar_ref[...] = jnp.concatenate([t.carry[h] for h in halves], axis=1)
        if want_out:
            hs = jnp.concatenate([t.hs[h] for h in halves], axis=1)
            t.y_lru = (hs * jax.nn.gelu(t.p_lru[:, LRU_W:])).astype(bf16)

    steps = [(conv, STEP_COST["conv"])]
    for half in halves:
        steps += [(gates(half), STEP_COST["gates"]), (scan(half), STEP_COST["scan"])]
    return steps + [(finish, STEP_COST["lru_finish"])]


def _phase_gla(t, p, chunk, scat_ref, tri=None, want_out=True):
    outs = []

    def prepare():
        q, k = t.p_qk[:, :GLA_DK], t.p_qk[:, GLA_DK:]
        n_chunks = q.shape[0] // chunk
        z = _dot(t.p_g[:, GLA_DV:].astype(bf16), p.al_w[...]) + p.al_b[...]
        g = jax.nn.log_sigmoid(z) * (1.0 / GLA_TAU)
        b = _chunk_cumsum(g, chunk) if tri is None else _chunk_cumsum_mxu(g, tri)
        last_rows = [b[(c + 1) * chunk - 1:(c + 1) * chunk] for c in range(n_chunks)]
        b_last = jnp.concatenate([jnp.broadcast_to(r, (chunk, GLA_DK)) for r in last_rows], axis=0)
        qs = q * (GLA_DKH ** -0.5)
        t.q_state = (qs * jnp.exp(b)).astype(bf16)
        t.k_end = (k * jnp.exp(b_last - b)).astype(bf16)
        t.q_end = (qs * jnp.exp(b - b_last)).astype(bf16)
        t.vb = t.p_v.astype(bf16)
        pad = jnp.zeros((SUBLANES - n_chunks, GLA_DK), f32)
        t.chunk_decay = jnp.exp(jnp.concatenate(last_rows + [pad], axis=0)).T
        lane = lax.broadcasted_iota(jnp.int32, (chunk, GLA_DK), 1)
        t.head_masks = [(lane // GLA_DKH) == h for h in range(GLA_H)]
        t_idx = lax.broadcasted_iota(jnp.int32, (GLA_H * chunk, chunk), 0) % chunk
        s_idx = lax.broadcasted_iota(jnp.int32, (GLA_H * chunk, chunk), 1)
        t.causal = s_idx <= t_idx

    def one_chunk(c):
        def step():
            sl = slice(c * chunk, (c + 1) * chunk)
            scores = _dot_nt(_head_stack(t.q_end[sl], t.head_masks), t.k_end[sl])
            probs = jnp.where(t.causal, scores, 0.0).astype(bf16)
            state = scat_ref[...]
            o_state = _dot(_head_stack(t.q_state[sl], t.head_masks), state.astype(bf16))
            v_heads = [t.vb[sl, h * GLA_DVH:(h + 1) * GLA_DVH] for h in range(GLA_H)]
            outs.append(jnp.concatenate(
                [o_state[h * chunk:(h + 1) * chunk] + _dot(probs[h * chunk:(h + 1) * chunk], v_heads[h])
                 for h in range(GLA_H)], axis=1))
            d_state = _dot_tn(_head_stack(t.k_end[sl], t.head_masks), jnp.concatenate(v_heads, axis=0))
            scat_ref[...] = t.chunk_decay[:, c:c + 1] * state + d_state
        return step

    def finish():
        if want_out:
            t.y_gla = _gla_finish(jnp.concatenate(outs, axis=0), t.p_g[:, :GLA_DV], p.gn_g[...]).astype(bf16)

    return ([(prepare, STEP_COST["gla_prepare"])]
            + [(one_chunk(c), STEP_COST["gla_chunk"]) for c in range(t.rows // chunk)]
            + [(finish, STEP_COST["gla_finish"])])


def _phase_out(t, p):
    mix = []

    def project(j):
        def step():
            cols = slice(j * MXU_DIM, (j + 1) * MXU_DIM)
            mix.append(_dot(t.y_lru, p.w_out[0:LRU_W, cols]) + _dot(t.y_gla, p.w_out[LRU_W:, cols]))
        return step

    def norm():
        t.h1 = _layer_norm(DEEPNORM_ALPHA * t.h + jnp.concatenate(mix, axis=1), p.ln1_g[...], p.ln1_b[...])
        t.h1b = t.h1.astype(bf16)

    return ([(project(j), STEP_COST["out_cols"]) for j in range(D_MODEL // MXU_DIM)]
            + [(norm, STEP_COST["out_norm"])])


def _phase_ffn(t, fp, store):
    acts, outs = [], []

    def up(n):
        def step():
            cols = slice(n * MXU_DIM, (n + 1) * MXU_DIM)
            gate = _dot(t.h1b, fp.w_gate[:, cols])
            acts.append((jax.nn.silu(gate) * _dot(t.h1b, fp.w_up[:, cols])).astype(bf16))
        return step

    def down(j):
        def step():
            if j == 0:
                t.act = jnp.concatenate(acts, axis=1)
            outs.append(_dot(t.act, fp.w_down[:, j * MXU_DIM:(j + 1) * MXU_DIM]))
        return step

    def norm():
        ffn = jnp.concatenate(outs, axis=1)
        store(_layer_norm(DEEPNORM_ALPHA * t.h1 + ffn, fp.ln2_g[...], fp.ln2_b[...]))

    return ([(up(n), STEP_COST["ffn_up"]) for n in range(D_FF // MXU_DIM)]
            + [(down(j), STEP_COST["ffn_down"]) for j in range(D_MODEL // MXU_DIM)]
            + [(norm, STEP_COST["ffn_norm"])])


def _run_interleaved(*phases):
    keyed = []
    for n, steps in enumerate(phases):
        total = sum(cost for _, cost in steps)
        done = 0.0
        for step, cost in steps:
            keyed.append(((done + 0.5 * cost) / total, n, step))
            done += cost
    for _, _, step in sorted(keyed, key=lambda e: e[:2]):
        step()


def _meta_kernel(meta_ref, *refs):
    p = MixerParams(*refs[:N_MIXER_PARAMS])
    xlc_ref, h_ref, s_ref, xl_scr = refs[N_MIXER_PARAMS:]
    xl_scr[...] = jnp.zeros_like(xl_scr)
    h_ref[...] = jnp.zeros_like(h_ref)
    s_ref[...] = jnp.zeros_like(s_ref)
    t = types.SimpleNamespace(rows=N_META)
    _run_interleaved(_phase_project(t, lambda: meta_ref[...], p))
    _run_interleaved(_phase_lru(t, p, xl_scr, h_ref, want_out=False))
    _run_interleaved(_phase_gla(t, p, N_META, s_ref, want_out=False))
    xlc_ref[...] = xl_scr[0:SUBLANES, :]


def _prompt_kernel(seq_subtiles, x_ref, xlc0_ref, h0_ref, s0_ref, tri_ref, *refs):
    p = MixerParams(*refs[:N_MIXER_PARAMS])
    fp = FfnParams(*refs[N_MIXER_PARAMS:N_MIXER_PARAMS + N_FFN_PARAMS])
    (y_ref, gla_ref, lru_ref, conv_ref, xl_scr, hcar_scr, scat_scr,
     c_h, c_lru, c_qk, c_v, c_g, c_h1) = refs[N_MIXER_PARAMS + N_FFN_PARAMS:]
    g = pl.program_id(0)
    tri = tri_ref[...]
    carried = dict(h=c_h, p_lru=c_lru, p_qk=c_qk, p_v=c_v, p_g=c_g)

    @pl.when(g == 0)
    def _():
        for ref in (c_h, c_lru, c_qk, c_v, c_g, c_h1, xl_scr, hcar_scr, scat_scr):
            ref[...] = jnp.zeros_like(ref)

    def new_tile(**values):
        return types.SimpleNamespace(rows=PROMPT_SUBTILE, **values)

    def mixers_and_out(t):
        return (_phase_lru(t, p, xl_scr, hcar_scr) + _phase_gla(t, p, GLA_CHUNK, scat_scr, tri)
                + _phase_out(t, p))

    def ffn(t, half):
        def store(v):
            y_ref[half * PROMPT_SUBTILE:(half + 1) * PROMPT_SUBTILE, :] = v
        return _phase_ffn(t, fp, store)

    def project(t, half):
        return _phase_project(t, lambda: x_ref[half * PROMPT_SUBTILE:(half + 1) * PROMPT_SUBTILE, :], p)

    def save(ref, t, name):
        def step():
            ref[...] = getattr(t, name)
        return (step, STEP_COST["save"])

    t_prev = new_tile(**{name: ref[...] for name, ref in carried.items()})
    h1_old = c_h1[...]
    t_ffn = new_tile(h1=h1_old, h1b=h1_old.astype(bf16))
    t_even = new_tile()
    _run_interleaved(mixers_and_out(t_prev), ffn(t_ffn, 0), project(t_even, 0))

    gla_ref[0] = scat_scr[...]
    lru_ref[0] = hcar_scr[...]
    conv_ref[0] = xl_scr[SUBLANES - (CONV_K - 1):SUBLANES, :]
    starts = g % (seq_subtiles // 2) == 0
    xl_scr[0:SUBLANES, :] = jnp.where(starts, xlc0_ref[...], xl_scr[0:SUBLANES, :])
    hcar_scr[...] = jnp.where(starts, h0_ref[...], hcar_scr[...])
    scat_scr[...] = jnp.where(starts, s0_ref[...], scat_scr[...])

    t_odd = new_tile()
    _run_interleaved(mixers_and_out(t_even) + [save(c_h1, t_even, "h1")],
                     ffn(t_prev, 1),
                     project(t_odd, 1) + [save(ref, t_odd, name) for name, ref in carried.items()])


def _sample_kernel(x_ref, conv_in_ref, lru_in_ref, s_in_ref, *refs):
    p = MixerParams(*refs[:N_MIXER_PARAMS])
    (h1_ref, s_out_ref, lru_out_ref, conv_out_ref,
     hln_scr, ylru_scr, q_scr, k_scr, eg_scr, v_scr, gg_scr, o_scr) = refs[N_MIXER_PARAMS:]
    i = pl.program_id(0)

    @pl.when(i == 0)
    def _():
        h = _layer_norm(x_ref[...], p.ln_g[...], p.ln_b[...])
        hln_scr[...] = h
        proj = _dot(h.astype(bf16), p.w_in[...])
        x_lru = proj[:, COL_XLRU:COL_GLRU]
        cw = p.conv_w[...]
        u = p.conv_b[...] + cw[3:4] * x_lru
        for j in range(CONV_K - 1):
            u = u + cw[j:j + 1] * conv_in_ref[:, j, :]
        for j in range(CONV_K - 2):
            conv_out_ref[:, j, :] = conv_in_ref[:, j + 1, :]
        conv_out_ref[:, CONV_K - 2, :] = x_lru
        h_in = lru_in_ref[...]
        halves = []
        for half in range(LRU_W // MXU_DIM):
            lo, hi = half * MXU_DIM, (half + 1) * MXU_DIM
            a, xin = _lru_gates(u[:, lo:hi], half, p)
            halves.append(a * h_in[:, lo:hi] + xin)
        h_new = jnp.concatenate(halves, axis=1)
        lru_out_ref[...] = h_new
        ylru_scr[...] = h_new * jax.nn.gelu(proj[:, COL_GLRU:COL_Q])
        z = _dot(proj[:, COL_ALR:COL_ALR + ALR_PAD].astype(bf16), p.al_w[...]) + p.al_b[...]
        eg_scr[...] = jnp.exp(jax.nn.log_sigmoid(z) * (1.0 / GLA_TAU))
        q_scr[...] = proj[:, COL_Q:COL_K] * (GLA_DKH ** -0.5)
        k_scr[...] = proj[:, COL_K:COL_V]
        v_scr[...] = proj[:, COL_V:COL_GGLA]
        gg_scr[...] = proj[:, COL_GGLA:COL_ALR]

    r0 = pl.multiple_of(i * SAMPLE_BLOCK, SAMPLE_BLOCK)
    q_t = q_scr[pl.ds(r0, SAMPLE_BLOCK), :].T
    k_t = k_scr[pl.ds(r0, SAMPLE_BLOCK), :].T
    e_t = eg_scr[pl.ds(r0, SAMPLE_BLOCK), :].T
    v_blk = v_scr[pl.ds(r0, SAMPLE_BLOCK), :]
    o_rows = []
    for j in range(SAMPLE_BLOCK):
        heads = []
        for h in range(GLA_H):
            ks = slice(h * GLA_DKH, (h + 1) * GLA_DKH)
            s_new = (e_t[ks, j:j + 1] * s_in_ref[j, h]
                     + k_t[ks, j:j + 1] * v_blk[j:j + 1, h * GLA_DVH:(h + 1) * GLA_DVH])
            s_out_ref[j, h] = s_new
            heads.append(jnp.sum(q_t[ks, j:j + 1] * s_new, axis=0, keepdims=True))
        o_rows.append(jnp.concatenate(heads, axis=1))
    o_scr[pl.ds(r0, SAMPLE_BLOCK), :] = jnp.concatenate(o_rows, axis=0)

    @pl.when(i == pl.num_programs(0) - 1)
    def _():
        y_gla = _gla_finish(o_scr[...], gg_scr[...], p.gn_g[...])
        y = jnp.concatenate([ylru_scr[...], y_gla], axis=1).astype(bf16)
        mix = _dot(y, p.w_out[...])
        h1_ref[...] = _layer_norm(DEEPNORM_ALPHA * hln_scr[...] + mix, p.ln1_g[...], p.ln1_b[...])


def _ffn_kernel(h_ref, *refs):
    fp = FfnParams(*refs[:N_FFN_PARAMS])
    out_ref, = refs[N_FFN_PARAMS:]
    h1 = h_ref[...]
    t = types.SimpleNamespace(h1=h1, h1b=h1.astype(bf16))

    def store(v):
        out_ref[...] = v

    _run_interleaved(_phase_ffn(t, fp, store))


def _block_diag(blocks):
    n, r, c = blocks.shape
    eye = jnp.eye(n, dtype=blocks.dtype)
    return (eye[:, None, :, None] * blocks[:, :, None, :]).reshape(n * r, n * c)


def _const_spec(arr):
    zeros = (0,) * arr.ndim
    return pl.BlockSpec(arr.shape, lambda *_: zeros, pipeline_mode=pl.Buffered(1))


def _mixer_params(ln_in_g, ln_in_b, w_in, conv_w, conv_b, ga_w, ga_b, gx_w, gx_b, lam, al_w, al_b, gn_g, w_out,
                  ln1_g, ln1_b):
    row = lambda a: a.reshape(1, -1).astype(f32)
    w_in_p = jnp.pad(w_in, ((0, 0), (0, IN_COLS_PAD - IN_COLS))).astype(bf16)
    per_half = MXU_DIM // LRU_BLK
    w_gate = jnp.stack([
        jnp.concatenate([_block_diag(ga_w[c * per_half:(c + 1) * per_half]),
                         _block_diag(gx_w[c * per_half:(c + 1) * per_half])], axis=1)
        for c in range(LRU_W // MXU_DIM)]).astype(bf16)
    al_w_p = jnp.pad(al_w, ((0, ALR_PAD - GLA_RANK), (0, 0))).astype(bf16)
    return MixerParams(row(ln_in_g), row(ln_in_b), w_in_p, conv_w.astype(f32), row(conv_b), w_gate, row(ga_b),
                       row(gx_b), row(lam), al_w_p, row(al_b), row(gn_g), w_out.astype(bf16), row(ln1_g), row(ln1_b))


def _meta_call(meta_tokens, params):
    out_shape = (jax.ShapeDtypeStruct((SUBLANES, LRU_W), f32),
                 jax.ShapeDtypeStruct((1, LRU_W), f32),
                 jax.ShapeDtypeStruct((GLA_DK, GLA_DVH), f32))
    return pl.pallas_call(
        _meta_kernel, out_shape=out_shape, name="meta_state",
        scratch_shapes=[pltpu.VMEM((N_META + SUBLANES, LRU_W), f32)],
        compiler_params=pltpu.CompilerParams(vmem_limit_bytes=VMEM_LIMIT_BYTES),
    )(meta_tokens, *params)


def _prompt_call(x, xlc0, h0, s0, params, ffn_params):
    batch, seq, _ = x.shape
    assert PROMPT_TILE == 2 * PROMPT_SUBTILE and seq % PROMPT_TILE == 0 and PROMPT_SUBTILE % GLA_CHUNK == 0
    seq_subtiles = seq // PROMPT_SUBTILE
    n_blocks = batch * seq // PROMPT_TILE
    pos = jnp.arange(PROMPT_SUBTILE)
    tri = ((pos[:, None] >= pos[None, :])
           & (pos[:, None] // GLA_CHUNK == pos[None, :] // GLA_CHUNK)).astype(bf16)
    x_map = lambda g: (jnp.minimum(g, n_blocks - 1), 0)
    y_map = lambda g: (jnp.maximum(g - 1, 0), 0)
    state_map = lambda g: (jnp.maximum(2 * g - 1, 0) // seq_subtiles, 0, 0)
    out_shape = (jax.ShapeDtypeStruct((batch * seq, D_MODEL), f32),
                 jax.ShapeDtypeStruct((batch, GLA_DK, GLA_DVH), f32),
                 jax.ShapeDtypeStruct((batch, 1, LRU_W), f32),
                 jax.ShapeDtypeStruct((batch, CONV_K - 1, LRU_W), f32))
    sub = lambda cols: pltpu.VMEM((PROMPT_SUBTILE, cols), f32)
    y, gla, lru, conv = pl.pallas_call(
        functools.partial(_prompt_kernel, seq_subtiles), out_shape=out_shape, name="prompt_mixer",
        grid=(n_blocks + 1,),
        in_specs=[pl.BlockSpec((PROMPT_TILE, D_MODEL), x_map),
                  _const_spec(xlc0), _const_spec(h0), _const_spec(s0), _const_spec(tri)]
                 + [_const_spec(a) for a in params] + [_const_spec(a) for a in ffn_params],
        out_specs=(pl.BlockSpec((PROMPT_TILE, D_MODEL), y_map),
                   pl.BlockSpec((1, GLA_DK, GLA_DVH), state_map),
                   pl.BlockSpec((1, 1, LRU_W), state_map),
                   pl.BlockSpec((1, CONV_K - 1, LRU_W), state_map)),
        scratch_shapes=[pltpu.VMEM((PROMPT_SUBTILE + SUBLANES, LRU_W), f32),
                        pltpu.VMEM((1, LRU_W), f32),
                        pltpu.VMEM((GLA_DK, GLA_DVH), f32),
                        sub(D_MODEL), sub(COL_Q - COL_XLRU), sub(COL_V - COL_Q), sub(COL_GGLA - COL_V),
                        sub(IN_COLS_PAD - COL_GGLA), sub(D_MODEL)],
        compiler_params=pltpu.CompilerParams(dimension_semantics=("arbitrary",),
                                             vmem_limit_bytes=VMEM_LIMIT_BYTES),
    )(x.reshape(batch * seq, D_MODEL), xlc0, h0, s0, tri, *params, *ffn_params)
    return y.reshape(batch, seq, D_MODEL), gla, lru, conv


def _sample_call(x, conv_in, lru_in, gla_in, params):
    n = x.shape[0]
    assert n % SAMPLE_BLOCK == 0
    state_block = (SAMPLE_BLOCK, GLA_H, GLA_DKH, GLA_DVH)
    state_map = lambda i: (i, 0, 0, 0)
    out_shape = (jax.ShapeDtypeStruct((n, D_MODEL), f32),
                 jax.ShapeDtypeStruct(gla_in.shape, f32),
                 jax.ShapeDtypeStruct((n, LRU_W), f32),
                 jax.ShapeDtypeStruct((n, CONV_K - 1, LRU_W), f32))
    full = lambda shape: pl.BlockSpec(shape, lambda i: (0,) * len(shape))
    return pl.pallas_call(
        _sample_kernel, out_shape=out_shape, name="sample_mixer",
        grid=(n // SAMPLE_BLOCK,),
        in_specs=[_const_spec(x), _const_spec(conv_in), _const_spec(lru_in),
                  pl.BlockSpec(state_block, state_map)] + [_const_spec(a) for a in params],
        out_specs=(full((n, D_MODEL)), pl.BlockSpec(state_block, state_map), full((n, LRU_W)),
                   full((n, CONV_K - 1, LRU_W))),
        scratch_shapes=[pltpu.VMEM((n, D_MODEL), f32), pltpu.VMEM((n, LRU_W), f32),
                        pltpu.VMEM((n, GLA_DK), f32), pltpu.VMEM((n, GLA_DK), f32), pltpu.VMEM((n, GLA_DK), f32),
                        pltpu.VMEM((n, GLA_DV), f32), pltpu.VMEM((n, GLA_DV), f32), pltpu.VMEM((n, GLA_DV), f32)],
        compiler_params=pltpu.CompilerParams(dimension_semantics=("arbitrary",),
                                             vmem_limit_bytes=VMEM_LIMIT_BYTES),
    )(x, conv_in, lru_in, gla_in, *params)


def _ffn_call(h, ffn_params):
    return pl.pallas_call(
        _ffn_kernel, out_shape=jax.ShapeDtypeStruct(h.shape, f32), name="sample_ffn",
        compiler_params=pltpu.CompilerParams(vmem_limit_bytes=VMEM_LIMIT_BYTES),
    )(h, *ffn_params)


def kernel(x_prompt, x_sample, state_gla, state_lru, state_conv, meta_tokens, ln_in_g, ln_in_b, w_in, conv_w, conv_b, lru_gate_a_w, lru_gate_a_b, lru_gate_x_w, lru_gate_x_b, lru_lambda, gla_alpha_w, gla_alpha_b, gla_norm_g, w_out, ln1_g, ln1_b, w_ffn_gate, w_ffn_up, w_ffn_down, ln2_g, ln2_b):
    assert w_in.shape[0] == 1 and x_sample.shape[1] == 1, "one layer, one decode token per sample sequence"
    batch, seq, _ = x_prompt.shape
    n_sample = x_sample.shape[0]
    params = _mixer_params(ln_in_g, ln_in_b, w_in[0], conv_w[0], conv_b[0], lru_gate_a_w[0], lru_gate_a_b[0],
                           lru_gate_x_w[0], lru_gate_x_b[0], lru_lambda[0], gla_alpha_w[0], gla_alpha_b[0],
                           gla_norm_g[0], w_out[0], ln1_g[0], ln1_b[0])
    ffn_params = FfnParams(w_ffn_gate[0].astype(bf16), w_ffn_up[0].astype(bf16), w_ffn_down[0].astype(bf16),
                           ln2_g[0].reshape(1, -1), ln2_b[0].reshape(1, -1))

    xlc0, h0, s0 = _meta_call(meta_tokens, params)
    y_p, gla_p, lru_p, conv_p = _prompt_call(x_prompt, xlc0, h0, s0, params, ffn_params)
    h1_s, gla_s, lru_s, conv_s = _sample_call(x_sample[:, 0, :], state_conv[0], state_lru[0], state_gla[0], params)
    y_s = _ffn_call(h1_s, ffn_params)
    return (y_p,
            y_s.reshape(n_sample, 1, D_MODEL),
            gla_p.reshape(1, batch, GLA_H, GLA_DKH, GLA_DVH),
            lru_p.reshape(1, batch, LRU_W),
            conv_p[None],
            gla_s[None],
            lru_s[None],
            conv_s[None])
```

```python
import math
import types
from typing import NamedTuple

import jax
import jax.numpy as jnp
from jax import lax
from jax.experimental import pallas as pl
from jax.experimental.pallas import tpu as pltpu

f32 = jnp.float32
bf16 = jnp.bfloat16

D_MODEL = 1024
N_META = 16
LRU_W = 512
LRU_BLOCKS = 8
LRU_BLK = LRU_W // LRU_BLOCKS
LRU_C = 8.0
CONV_K = 4
GLA_H = 4
GLA_DKH = 64
GLA_DVH = 128
GLA_DK = GLA_H * GLA_DKH
GLA_DV = GLA_H * GLA_DVH
GLA_RANK = 16
GLA_TAU = 16.0
D_FF = 2816
DEEPNORM_ALPHA = 2.0 ** 0.25
LN_EPS = 1e-5
RMS_EPS = 1e-6
GELU_C = math.sqrt(2.0 / math.pi)
GELU_CUBIC = 0.044715
LOG2_E = math.log2(math.e)
F32_TINY = float(jnp.finfo(jnp.float32).tiny)

COL_XLRU = 0
COL_GLRU = COL_XLRU + LRU_W
COL_Q = COL_GLRU + LRU_W
COL_K = COL_Q + GLA_DK
COL_V = COL_K + GLA_DK
COL_GGLA = COL_V + GLA_DV
COL_ALR = COL_GGLA + GLA_DV
IN_COLS = COL_ALR + GLA_RANK

LANES = 128
SUBLANES = 8
MXU_DIM = 256
ALR_PAD = LANES
IN_COLS_PAD = COL_ALR + ALR_PAD
VMEM_LIMIT_BYTES = 56 * 1024 * 1024

PROMPT_TILE = 512
PROMPT_SUBTILE = 256
GLA_CHUNK = 64
SAMPLE_BLOCK = 2 * SUBLANES


class MixerParams(NamedTuple):
    ln_g: jax.Array
    ln_b: jax.Array
    w_in: jax.Array
    conv_w: jax.Array
    conv_b: jax.Array
    w_gate: jax.Array
    gate_a_b: jax.Array
    gate_x_b: jax.Array
    lam: jax.Array
    al_w: jax.Array
    al_b: jax.Array
    gn_g: jax.Array
    w_out: jax.Array
    ln1_g: jax.Array
    ln1_b: jax.Array


N_MIXER_PARAMS = len(MixerParams._fields)


class FfnParams(NamedTuple):
    w_gate: jax.Array
    w_up: jax.Array
    w_down: jax.Array
    ln2_g: jax.Array
    ln2_b: jax.Array


N_FFN_PARAMS = len(FfnParams._fields)


def _layer_norm(x, g, b):
    mu = jnp.mean(x, -1, keepdims=True)
    xc = x - mu
    var = jnp.mean(xc * xc, -1, keepdims=True)
    return xc * lax.rsqrt(var + LN_EPS) * g + b


def _dot(a, b):
    return jnp.dot(a, b, preferred_element_type=f32)


def _dot_nt(a, b):
    return lax.dot_general(a, b, (((1,), (1,)), ((), ())), preferred_element_type=f32)


def _dot_tn(a, b):
    return lax.dot_general(a, b, (((0,), (0,)), ((), ())), preferred_element_type=f32)


def _sqrt_nonneg(y):
    return y * lax.rsqrt(jnp.maximum(y, F32_TINY))


def _gelu_tanh(x):
    k = -2.0 * GELU_C * LOG2_E
    return x * (1.0 / (1.0 + jnp.exp2(x * (k + (k * GELU_CUBIC) * (x * x)))))


def _gla_log_decay(z):
    scale = 1.0 / GLA_TAU
    return (jnp.minimum(z, 0.0) * scale
            - jnp.log2(1.0 + jnp.exp2(jnp.abs(z) * (-LOG2_E))) * (scale / LOG2_E))


def _lru_gates(u_half, half, p):
    lo = half * MXU_DIM
    hi = lo + MXU_DIM
    gates = _dot(u_half.astype(bf16), p.w_gate[half])
    r = jax.nn.sigmoid(gates[:, :MXU_DIM] + p.gate_a_b[:, lo:hi])
    i = jax.nn.sigmoid(gates[:, MXU_DIM:] + p.gate_x_b[:, lo:hi])
    neg_log_a = r * (LRU_C * jax.nn.softplus(-p.lam[:, lo:hi]))
    a = jnp.exp2(neg_log_a * (-LOG2_E))
    gain_sq = jnp.tanh(neg_log_a) * (a * a + 1.0)
    return a, _sqrt_nonneg(gain_sq) * (i * u_half)


def _scan_groups(a, x):
    rows, width = a.shape
    groups = rows // SUBLANES
    a3 = a.reshape(groups, SUBLANES, width)
    x3 = x.reshape(groups, SUBLANES, width)
    sub = lax.broadcasted_iota(jnp.int32, (groups, SUBLANES, width), 1)
    shift = 1
    while shift < SUBLANES:
        keep = sub >= shift
        a_prev = jnp.where(keep, pltpu.roll(a3, shift, 1), 1.0)
        x_prev = jnp.where(keep, pltpu.roll(x3, shift, 1), 0.0)
        x3 = x3 + a3 * x_prev
        a3 = a3 * a_prev
        shift *= 2
    return a3.reshape(rows, width), x3.reshape(rows, width)


def _lru_scan(a, x, h_in):
    rows = a.shape[0]
    big_a, big_x = _scan_groups(a, x)
    carry = h_in
    out = []
    for g in range(rows // SUBLANES):
        sl = slice(g * SUBLANES, (g + 1) * SUBLANES)
        hg = big_a[sl] * carry + big_x[sl]
        carry = hg[SUBLANES - 1:SUBLANES]
        out.append(hg)
    return jnp.concatenate(out, axis=0), carry


def _chunk_cumsum(g, chunk):
    row = lax.broadcasted_iota(jnp.int32, g.shape, 0) % chunk
    shift = 1
    while shift < chunk:
        g = g + jnp.where(row >= shift, pltpu.roll(g, shift, 0), 0.0)
        shift *= 2
    return g


def _chunk_cumsum_mxu(g, tri):
    hi = g.astype(bf16)
    lo = (g - hi.astype(f32)).astype(bf16)
    return _dot(tri, hi) + _dot(tri, lo)


def _head_stack(x, head_masks):
    zero = jnp.zeros_like(x)
    return jnp.concatenate([jnp.where(m, x, zero) for m in head_masks], axis=0)


def _gla_finish(o, g_gla, gn_g):
    heads = []
    for h in range(GLA_H):
        oh = o[:, h * GLA_DVH:(h + 1) * GLA_DVH]
        heads.append(oh * lax.rsqrt(jnp.mean(oh * oh, -1, keepdims=True) + RMS_EPS))
    return jnp.concatenate(heads, axis=1) * gn_g * jax.nn.silu(g_gla)


STEP_COST = dict(
    in_norm=0.6, in_lru=1.0, in_qk=0.5, in_v=0.5, in_g=0.6,
    conv=0.9, gates=0.7, scan=0.8, lru_finish=0.5,
    gla_prepare=0.9, gla_chunk=0.4, gla_finish=0.5,
    out_cols=0.25, out_norm=0.5,
    ffn_up=0.5, ffn_down=0.7, ffn_norm=0.5,
)


def _phase_project(t, x_of, p):
    def norm():
        t.h = _layer_norm(x_of(), p.ln_g[...], p.ln_b[...])
        t.hb = t.h.astype(bf16)

    def lru():
        t.p_lru = _dot(t.hb, p.w_in[:, COL_XLRU:COL_Q])

    def qk():
        t.p_qk = _dot(t.hb, p.w_in[:, COL_Q:COL_V])

    def v():
        t.p_v = _dot(t.hb, p.w_in[:, COL_V:COL_GGLA])

    def g():
        t.p_g = _dot(t.hb, p.w_in[:, COL_GGLA:IN_COLS_PAD])

    c = STEP_COST
    return [(norm, c["in_norm"]), (lru, c["in_lru"]), (qk, c["in_qk"]), (v, c["in_v"]), (g, c["in_g"])]


def _phase_lru(t, p, xl_ref, hcar_ref, want_out=True):
    halves = range(LRU_W // MXU_DIM)
    t.hs, t.carry, t.gates = {}, {}, {}

    def conv():
        x_lru = t.p_lru[:, :LRU_W]
        rows = x_lru.shape[0]
        xl_ref[SUBLANES:SUBLANES + rows, :] = x_lru
        cw = p.conv_w[...]
        u = p.conv_b[...] + cw[CONV_K - 1:CONV_K] * x_lru
        for j in range(CONV_K - 1):
            start = SUBLANES - (CONV_K - 1) + j
            u = u + cw[j:j + 1] * xl_ref[start:start + rows, :]
        xl_ref[0:SUBLANES, :] = xl_ref[rows:rows + SUBLANES, :]
        t.u = u
        t.h_in = hcar_ref[...]

    def gates(half):
        def step():
            t.gates[half] = _lru_gates(t.u[:, half * MXU_DIM:(half + 1) * MXU_DIM], half, p)
        return step

    def scan(half):
        def step():
            a, xin = t.gates[half]
            t.hs[half], t.carry[half] = _lru_scan(a, xin, t.h_in[:, half * MXU_DIM:(half + 1) * MXU_DIM])
        return step

    def finish():
        hcar_ref[...] = jnp.concatenate([t.carry[h] for h in halves], axis=1)
        if want_out:
            hs = jnp.concatenate([t.hs[h] for h in halves], axis=1)
            t.y_lru = (hs * _gelu_tanh(t.p_lru[:, LRU_W:])).astype(bf16)

    steps = [(conv, STEP_COST["conv"])]
    for half in halves:
        steps += [(gates(half), STEP_COST["gates"]), (scan(half), STEP_COST["scan"])]
    return steps + [(finish, STEP_COST["lru_finish"])]


def _phase_gla(t, p, chunk, scat_ref, tri=None, want_out=True):
    outs = []

    def prepare():
        q, k = t.p_qk[:, :GLA_DK], t.p_qk[:, GLA_DK:]
        n_chunks = q.shape[0] // chunk
        z = _dot(t.p_g[:, GLA_DV:].astype(bf16), p.al_w[...]) + p.al_b[...]
        g = _gla_log_decay(z)
        b = _chunk_cumsum(g, chunk) if tri is None else _chunk_cumsum_mxu(g, tri)
        last_rows = [b[(c + 1) * chunk - 1:(c + 1) * chunk] for c in range(n_chunks)]
        b_last = jnp.concatenate([jnp.broadcast_to(r, (chunk, GLA_DK)) for r in last_rows], axis=0)
        qs = q * (GLA_DKH ** -0.5)
        t.q_state = (qs * jnp.exp(b)).astype(bf16)
        t.k_end = (k * jnp.exp(b_last - b)).astype(bf16)
        t.q_end = (qs * jnp.exp(b - b_last)).astype(bf16)
        t.vb = t.p_v.astype(bf16)
        pad = jnp.zeros((SUBLANES - n_chunks, GLA_DK), f32)
        t.chunk_decay = jnp.exp(jnp.concatenate(last_rows + [pad], axis=0)).T
        lane = lax.broadcasted_iota(jnp.int32, (chunk, GLA_DK), 1)
        t.head_masks = [(lane // GLA_DKH) == h for h in range(GLA_H)]
        t_idx = lax.broadcasted_iota(jnp.int32, (GLA_H * chunk, chunk), 0) % chunk
        s_idx = lax.broadcasted_iota(jnp.int32, (GLA_H * chunk, chunk), 1)
        t.causal = s_idx <= t_idx

    def one_chunk(c):
        def step():
            sl = slice(c * chunk, (c + 1) * chunk)
            scores = _dot_nt(_head_stack(t.q_end[sl], t.head_masks), t.k_end[sl])
            probs = jnp.where(t.causal, scores, 0.0).astype(bf16)
            state = scat_ref[...]
            o_state = _dot(_head_stack(t.q_state[sl], t.head_masks), state.astype(bf16))
            v_heads = [t.vb[sl, h * GLA_DVH:(h + 1) * GLA_DVH] for h in range(GLA_H)]
            outs.append(jnp.concatenate(
                [o_state[h * chunk:(h + 1) * chunk] + _dot(probs[h * chunk:(h + 1) * chunk], v_heads[h])
                 for h in range(GLA_H)], axis=1))
            d_state = _dot_tn(_head_stack(t.k_end[sl], t.head_masks), jnp.concatenate(v_heads, axis=0))
            scat_ref[...] = t.chunk_decay[:, c:c + 1] * state + d_state
        return step

    def finish():
        if want_out:
            t.y_gla = _gla_finish(jnp.concatenate(outs, axis=0), t.p_g[:, :GLA_DV], p.gn_g[...]).astype(bf16)

    return ([(prepare, STEP_COST["gla_prepare"])]
            + [(one_chunk(c), STEP_COST["gla_chunk"]) for c in range(t.rows // chunk)]
            + [(finish, STEP_COST["gla_finish"])])


def _phase_out(t, p):
    mix = []

    def project(j):
        def step():
            cols = slice(j * MXU_DIM, (j + 1) * MXU_DIM)
            mix.append(_dot(t.y_lru, p.w_out[0:LRU_W, cols]) + _dot(t.y_gla, p.w_out[LRU_W:, cols]))
        return step

    def norm():
        t.h1 = _layer_norm(DEEPNORM_ALPHA * t.h + jnp.concatenate(mix, axis=1), p.ln1_g[...], p.ln1_b[...])
        t.h1b = t.h1.astype(bf16)

    return ([(project(j), STEP_COST["out_cols"]) for j in range(D_MODEL // MXU_DIM)]
            + [(norm, STEP_COST["out_norm"])])


def _phase_ffn(t, fp, store):
    acts, outs = [], []

    def up(n):
        def step():
            cols = slice(n * MXU_DIM, (n + 1) * MXU_DIM)
            gate = _dot(t.h1b, fp.w_gate[:, cols])
            acts.append((jax.nn.silu(gate) * _dot(t.h1b, fp.w_up[:, cols])).astype(bf16))
        return step

    def down(j):
        def step():
            if j == 0:
                t.act = jnp.concatenate(acts, axis=1)
            outs.append(_dot(t.act, fp.w_down[:, j * MXU_DIM:(j + 1) * MXU_DIM]))
        return step

    def norm():
        ffn = jnp.concatenate(outs, axis=1)
        store(_layer_norm(DEEPNORM_ALPHA * t.h1 + ffn, fp.ln2_g[...], fp.ln2_b[...]))

    return ([(up(n), STEP_COST["ffn_up"]) for n in range(D_FF // MXU_DIM)]
            + [(down(j), STEP_COST["ffn_down"]) for j in range(D_MODEL // MXU_DIM)]
            + [(norm, STEP_COST["ffn_norm"])])


def _run_interleaved(*phases):
    keyed = []
    for n, steps in enumerate(phases):
        total = sum(cost for _, cost in steps)
        done = 0.0
        for step, cost in steps:
            keyed.append(((done + 0.5 * cost) / total, n, step))
            done += cost
    for _, _, step in sorted(keyed, key=lambda e: e[:2]):
        step()


def _meta_kernel(meta_ref, *refs):
    p = MixerParams(*refs[:N_MIXER_PARAMS])
    xlc_ref, h_ref, s_ref, xl_scr = refs[N_MIXER_PARAMS:]
    xl_scr[...] = jnp.zeros_like(xl_scr)
    h_ref[...] = jnp.zeros_like(h_ref)
    s_ref[...] = jnp.zeros_like(s_ref)
    t = types.SimpleNamespace(rows=N_META)
    _run_interleaved(_phase_project(t, lambda: meta_ref[...], p))
    _run_interleaved(_phase_lru(t, p, xl_scr, h_ref, want_out=False))
    _run_interleaved(_phase_gla(t, p, N_META, s_ref, want_out=False))
    xlc_ref[...] = xl_scr[0:SUBLANES, :]


def _prompt_kernel(x_ref, xlc0_ref, h0_ref, s0_ref, tri_ref, *refs):
    p = MixerParams(*refs[:N_MIXER_PARAMS])
    fp = FfnParams(*refs[N_MIXER_PARAMS:N_MIXER_PARAMS + N_FFN_PARAMS])
    y_ref, gla_ref, lru_ref, conv_ref, xl_scr, hcar_scr, scat_scr = refs[N_MIXER_PARAMS + N_FFN_PARAMS:]
    step = pl.program_id(1)

    @pl.when(step == 0)
    def _():
        xl_scr[0:SUBLANES, :] = xlc0_ref[...]
        hcar_scr[...] = h0_ref[...]
        scat_scr[...] = s0_ref[...]

    n_sub = PROMPT_TILE // PROMPT_SUBTILE
    tiles = [types.SimpleNamespace(rows=PROMPT_SUBTILE) for _ in range(n_sub)]
    tri = tri_ref[...]

    def rows(s):
        return slice(s * PROMPT_SUBTILE, (s + 1) * PROMPT_SUBTILE)

    def project(s):
        return _phase_project(tiles[s], lambda: x_ref[0, rows(s), :], p)

    def out(s):
        def store(v):
            y_ref[0, rows(s), :] = v
        return _phase_out(tiles[s], p) + _phase_ffn(tiles[s], fp, store)

    first = project(0)
    lru_ready = 2
    _run_interleaved(first[:lru_ready])
    for s in range(n_sub):
        mixers = (_phase_lru(tiles[s], p, xl_scr, hcar_scr)
                  + _phase_gla(tiles[s], p, GLA_CHUNK, scat_scr, tri))
        others = (first[lru_ready:] if s == 0 else []) + (project(s + 1) if s + 1 < n_sub else [])
        _run_interleaved(mixers, *([others] if others else []), *([out(s - 1)] if s >= 1 else []))
    _run_interleaved(out(n_sub - 1))

    @pl.when(step == pl.num_programs(1) - 1)
    def _():
        gla_ref[0] = scat_scr[...]
        lru_ref[0] = hcar_scr[...]
        conv_ref[0] = xl_scr[SUBLANES - (CONV_K - 1):SUBLANES, :]


def _sample_kernel(x_ref, conv_in_ref, lru_in_ref, s_in_ref, *refs):
    p = MixerParams(*refs[:N_MIXER_PARAMS])
    (h1_ref, s_out_ref, lru_out_ref, conv_out_ref,
     hln_scr, ylru_scr, q_scr, k_scr, eg_scr, v_scr, gg_scr, o_scr) = refs[N_MIXER_PARAMS:]
    i = pl.program_id(0)

    @pl.when(i == 0)
    def _():
        h = _layer_norm(x_ref[...], p.ln_g[...], p.ln_b[...])
        hln_scr[...] = h
        proj = _dot(h.astype(bf16), p.w_in[...])
        x_lru = proj[:, COL_XLRU:COL_GLRU]
        cw = p.conv_w[...]
        u = p.conv_b[...] + cw[3:4] * x_lru
        for j in range(CONV_K - 1):
            u = u + cw[j:j + 1] * conv_in_ref[:, j, :]
        for j in range(CONV_K - 2):
            conv_out_ref[:, j, :] = conv_in_ref[:, j + 1, :]
        conv_out_ref[:, CONV_K - 2, :] = x_lru
        h_in = lru_in_ref[...]
        halves = []
        for half in range(LRU_W // MXU_DIM):
            lo, hi = half * MXU_DIM, (half + 1) * MXU_DIM
            a, xin = _lru_gates(u[:, lo:hi], half, p)
            halves.append(a * h_in[:, lo:hi] + xin)
        h_new = jnp.concatenate(halves, axis=1)
        lru_out_ref[...] = h_new
        ylru_scr[...] = h_new * _gelu_tanh(proj[:, COL_GLRU:COL_Q])
        z = _dot(proj[:, COL_ALR:COL_ALR + ALR_PAD].astype(bf16), p.al_w[...]) + p.al_b[...]
        eg_scr[...] = jnp.exp(_gla_log_decay(z))
        q_scr[...] = proj[:, COL_Q:COL_K] * (GLA_DKH ** -0.5)
        k_scr[...] = proj[:, COL_K:COL_V]
        v_scr[...] = proj[:, COL_V:COL_GGLA]
        gg_scr[...] = proj[:, COL_GGLA:COL_ALR]

    r0 = pl.multiple_of(i * SAMPLE_BLOCK, SAMPLE_BLOCK)
    q_t = q_scr[pl.ds(r0, SAMPLE_BLOCK), :].T
    k_t = k_scr[pl.ds(r0, SAMPLE_BLOCK), :].T
    e_t = eg_scr[pl.ds(r0, SAMPLE_BLOCK), :].T
    v_blk = v_scr[pl.ds(r0, SAMPLE_BLOCK), :]
    o_rows = []
    for j in range(SAMPLE_BLOCK):
        heads = []
        for h in range(GLA_H):
            ks = slice(h * GLA_DKH, (h + 1) * GLA_DKH)
            s_new = (e_t[ks, j:j + 1] * s_in_ref[j, h]
                     + k_t[ks, j:j + 1] * v_blk[j:j + 1, h * GLA_DVH:(h + 1) * GLA_DVH])
            s_out_ref[j, h] = s_new
            heads.append(jnp.sum(q_t[ks, j:j + 1] * s_new, axis=0, keepdims=True))
        o_rows.append(jnp.concatenate(heads, axis=1))
    o_scr[pl.ds(r0, SAMPLE_BLOCK), :] = jnp.concatenate(o_rows, axis=0)

    @pl.when(i == pl.num_programs(0) - 1)
    def _():
        y_gla = _gla_finish(o_scr[...], gg_scr[...], p.gn_g[...])
        y = jnp.concatenate([ylru_scr[...], y_gla], axis=1).astype(bf16)
        mix = _dot(y, p.w_out[...])
        h1_ref[...] = _layer_norm(DEEPNORM_ALPHA * hln_scr[...] + mix, p.ln1_g[...], p.ln1_b[...])


def _ffn_kernel(h_ref, *refs):
    fp = FfnParams(*refs[:N_FFN_PARAMS])
    out_ref, = refs[N_FFN_PARAMS:]
    h1 = h_ref[...]
    t = types.SimpleNamespace(h1=h1, h1b=h1.astype(bf16))

    def store(v):
        out_ref[...] = v

    _run_interleaved(_phase_ffn(t, fp, store))


def _block_diag(blocks):
    n, r, c = blocks.shape
    eye = jnp.eye(n, dtype=blocks.dtype)
    return (eye[:, None, :, None] * blocks[:, :, None, :]).reshape(n * r, n * c)


def _const_spec(arr):
    zeros = (0,) * arr.ndim
    return pl.BlockSpec(arr.shape, lambda *_: zeros, pipeline_mode=pl.Buffered(1))


def _mixer_params(ln_in_g, ln_in_b, w_in, conv_w, conv_b, ga_w, ga_b, gx_w, gx_b, lam, al_w, al_b, gn_g, w_out,
                  ln1_g, ln1_b):
    row = lambda a: a.reshape(1, -1).astype(f32)
    w_in_p = jnp.pad(w_in, ((0, 0), (0, IN_COLS_PAD - IN_COLS))).astype(bf16)
    per_half = MXU_DIM // LRU_BLK
    w_gate = jnp.stack([
        jnp.concatenate([_block_diag(ga_w[c * per_half:(c + 1) * per_half]),
                         _block_diag(gx_w[c * per_half:(c + 1) * per_half])], axis=1)
        for c in range(LRU_W // MXU_DIM)]).astype(bf16)
    al_w_p = jnp.pad(al_w, ((0, ALR_PAD - GLA_RANK), (0, 0))).astype(bf16)
    return MixerParams(row(ln_in_g), row(ln_in_b), w_in_p, conv_w.astype(f32), row(conv_b), w_gate, row(ga_b),
                       row(gx_b), row(lam), al_w_p, row(al_b), row(gn_g), w_out.astype(bf16), row(ln1_g), row(ln1_b))


def _meta_call(meta_tokens, params):
    out_shape = (jax.ShapeDtypeStruct((SUBLANES, LRU_W), f32),
                 jax.ShapeDtypeStruct((1, LRU_W), f32),
                 jax.ShapeDtypeStruct((GLA_DK, GLA_DVH), f32))
    return pl.pallas_call(
        _meta_kernel, out_shape=out_shape, name="meta_state",
        scratch_shapes=[pltpu.VMEM((N_META + SUBLANES, LRU_W), f32)],
        compiler_params=pltpu.CompilerParams(vmem_limit_bytes=VMEM_LIMIT_BYTES),
    )(meta_tokens, *params)


def _prompt_call(x, xlc0, h0, s0, params, ffn_params):
    batch, seq, _ = x.shape
    assert seq % PROMPT_TILE == 0 and PROMPT_TILE % PROMPT_SUBTILE == 0 and PROMPT_SUBTILE % GLA_CHUNK == 0
    n_tiles = seq // PROMPT_TILE
    pos = jnp.arange(PROMPT_SUBTILE)
    tri = ((pos[:, None] >= pos[None, :])
           & (pos[:, None] // GLA_CHUNK == pos[None, :] // GLA_CHUNK)).astype(bf16)
    seq_map = lambda b, t: (b, t, 0)
    state_map = lambda b, t: (b, 0, 0)
    out_shape = (jax.ShapeDtypeStruct((batch, seq, D_MODEL), f32),
                 jax.ShapeDtypeStruct((batch, GLA_DK, GLA_DVH), f32),
                 jax.ShapeDtypeStruct((batch, 1, LRU_W), f32),
                 jax.ShapeDtypeStruct((batch, CONV_K - 1, LRU_W), f32))
    return pl.pallas_call(
        _prompt_kernel, out_shape=out_shape, name="prompt_mixer",
        grid=(batch, n_tiles),
        in_specs=[pl.BlockSpec((1, PROMPT_TILE, D_MODEL), seq_map),
                  _const_spec(xlc0), _const_spec(h0), _const_spec(s0), _const_spec(tri)]
                 + [_const_spec(a) for a in params] + [_const_spec(a) for a in ffn_params],
        out_specs=(pl.BlockSpec((1, PROMPT_TILE, D_MODEL), seq_map),
                   pl.BlockSpec((1, GLA_DK, GLA_DVH), state_map),
                   pl.BlockSpec((1, 1, LRU_W), state_map),
                   pl.BlockSpec((1, CONV_K - 1, LRU_W), state_map)),
        scratch_shapes=[pltpu.VMEM((PROMPT_SUBTILE + SUBLANES, LRU_W), f32),
                        pltpu.VMEM((1, LRU_W), f32),
                        pltpu.VMEM((GLA_DK, GLA_DVH), f32)],
        compiler_params=pltpu.CompilerParams(dimension_semantics=("arbitrary", "arbitrary"),
                                             vmem_limit_bytes=VMEM_LIMIT_BYTES),
    )(x, xlc0, h0, s0, tri, *params, *ffn_params)


def _sample_call(x, conv_in, lru_in, gla_in, params):
    n = x.shape[0]
    assert n % SAMPLE_BLOCK == 0
    state_block = (SAMPLE_BLOCK, GLA_H, GLA_DKH, GLA_DVH)
    state_map = lambda i: (i, 0, 0, 0)
    out_shape = (jax.ShapeDtypeStruct((n, D_MODEL), f32),
                 jax.ShapeDtypeStruct(gla_in.shape, f32),
                 jax.ShapeDtypeStruct((n, LRU_W), f32),
                 jax.ShapeDtypeStruct((n, CONV_K - 1, LRU_W), f32))
    full = lambda shape: pl.BlockSpec(shape, lambda i: (0,) * len(shape))
    return pl.pallas_call(
        _sample_kernel, out_shape=out_shape, name="sample_mixer",
        grid=(n // SAMPLE_BLOCK,),
        in_specs=[_const_spec(x), _const_spec(conv_in), _const_spec(lru_in),
                  pl.BlockSpec(state_block, state_map)] + [_const_spec(a) for a in params],
        out_specs=(full((n, D_MODEL)), pl.BlockSpec(state_block, state_map), full((n, LRU_W)),
                   full((n, CONV_K - 1, LRU_W))),
        scratch_shapes=[pltpu.VMEM((n, D_MODEL), f32), pltpu.VMEM((n, LRU_W), f32),
                        pltpu.VMEM((n, GLA_DK), f32), pltpu.VMEM((n, GLA_DK), f32), pltpu.VMEM((n, GLA_DK), f32),
                        pltpu.VMEM((n, GLA_DV), f32), pltpu.VMEM((n, GLA_DV), f32), pltpu.VMEM((n, GLA_DV), f32)],
        compiler_params=pltpu.CompilerParams(dimension_semantics=("arbitrary",),
                                             vmem_limit_bytes=VMEM_LIMIT_BYTES),
    )(x, conv_in, lru_in, gla_in, *params)


def _ffn_call(h, ffn_params):
    return pl.pallas_call(
        _ffn_kernel, out_shape=jax.ShapeDtypeStruct(h.shape, f32), name="sample_ffn",
        compiler_params=pltpu.CompilerParams(vmem_limit_bytes=VMEM_LIMIT_BYTES),
    )(h, *ffn_params)


def kernel(x_prompt, x_sample, state_gla, state_lru, state_conv, meta_tokens, ln_in_g, ln_in_b, w_in, conv_w, conv_b, lru_gate_a_w, lru_gate_a_b, lru_gate_x_w, lru_gate_x_b, lru_lambda, gla_alpha_w, gla_alpha_b, gla_norm_g, w_out, ln1_g, ln1_b, w_ffn_gate, w_ffn_up, w_ffn_down, ln2_g, ln2_b):
    assert w_in.shape[0] == 1 and x_sample.shape[1] == 1, "one layer, one decode token per sample sequence"
    batch, seq, _ = x_prompt.shape
    n_sample = x_sample.shape[0]
    params = _mixer_params(ln_in_g, ln_in_b, w_in[0], conv_w[0], conv_b[0], lru_gate_a_w[0], lru_gate_a_b[0],
                           lru_gate_x_w[0], lru_gate_x_b[0], lru_lambda[0], gla_alpha_w[0], gla_alpha_b[0],
                           gla_norm_g[0], w_out[0], ln1_g[0], ln1_b[0])
    ffn_params = FfnParams(w_ffn_gate[0].astype(bf16), w_ffn_up[0].astype(bf16), w_ffn_down[0].astype(bf16),
                           ln2_g[0].reshape(1, -1), ln2_b[0].reshape(1, -1))

    xlc0, h0, s0 = _meta_call(meta_tokens, params)
    y_p, gla_p, lru_p, conv_p = _prompt_call(x_prompt, xlc0, h0, s0, params, ffn_params)
    h1_s, gla_s, lru_s, conv_s = _sample_call(x_sample[:, 0, :], state_conv[0], state_lru[0], state_gla[0], params)
    y_s = _ffn_call(h1_s, ffn_params)
    return (y_p,
            y_s.reshape(n_sample, 1, D_MODEL),
            gla_p.reshape(1, batch, GLA_H, GLA_DKH, GLA_DVH),
            lru_p.reshape(1, batch, LRU_W),
            conv_p[None],
            gla_s[None],
            lru_s[None],
            conv_s[None])
```

```python
import math
import types
from typing import NamedTuple

import jax
import jax.numpy as jnp
from jax import lax
from jax.experimental import pallas as pl
from jax.experimental.pallas import tpu as pltpu

f32 = jnp.float32
bf16 = jnp.bfloat16

D_MODEL = 1024
N_META = 16
LRU_W = 512
LRU_BLOCKS = 8
LRU_BLK = LRU_W // LRU_BLOCKS
LRU_C = 8.0
CONV_K = 4
GLA_H = 4
GLA_DKH = 64
GLA_DVH = 128
GLA_DK = GLA_H * GLA_DKH
GLA_DV = GLA_H * GLA_DVH
GLA_RANK = 16
GLA_TAU = 16.0
D_FF = 2816
DEEPNORM_ALPHA = 2.0 ** 0.25
LN_EPS = 1e-5
RMS_EPS = 1e-6
GELU_C = math.sqrt(2.0 / math.pi)
GELU_CUBIC = 0.044715
LOG2_E = math.log2(math.e)
F32_TINY = float(jnp.finfo(jnp.float32).tiny)

COL_XLRU = 0
COL_GLRU = COL_XLRU + LRU_W
COL_Q = COL_GLRU + LRU_W
COL_K = COL_Q + GLA_DK
COL_V = COL_K + GLA_DK
COL_GGLA = COL_V + GLA_DV
COL_ALR = COL_GGLA + GLA_DV
IN_COLS = COL_ALR + GLA_RANK

LANES = 128
SUBLANES = 8
MXU_DIM = 256
WIDE_STEP_COLS = 2 * MXU_DIM
ALR_PAD = LANES
IN_COLS_PAD = COL_ALR + ALR_PAD
VMEM_LIMIT_BYTES = 56 * 1024 * 1024

PROMPT_TILE = 512
PROMPT_SUBTILE = 256
GLA_CHUNK = 64
SAMPLE_BLOCK = 2 * SUBLANES


class MixerParams(NamedTuple):
    ln_g: jax.Array
    ln_b: jax.Array
    w_in: jax.Array
    conv_w: jax.Array
    conv_b: jax.Array
    w_gate: jax.Array
    gate_a_b: jax.Array
    gate_x_b: jax.Array
    lam: jax.Array
    al_w: jax.Array
    al_b: jax.Array
    gn_g: jax.Array
    w_out: jax.Array
    ln1_g: jax.Array
    ln1_b: jax.Array


N_MIXER_PARAMS = len(MixerParams._fields)


class FfnParams(NamedTuple):
    w_gate: jax.Array
    w_up: jax.Array
    w_down: jax.Array
    ln2_g: jax.Array
    ln2_b: jax.Array


N_FFN_PARAMS = len(FfnParams._fields)


def _layer_norm(x, g, b):
    mu = jnp.mean(x, -1, keepdims=True)
    xc = x - mu
    var = jnp.mean(xc * xc, -1, keepdims=True)
    return xc * lax.rsqrt(var + LN_EPS) * g + b


def _dot(a, b):
    return jnp.dot(a, b, preferred_element_type=f32)


def _dot_nt(a, b):
    return lax.dot_general(a, b, (((1,), (1,)), ((), ())), preferred_element_type=f32)


def _dot_tn(a, b):
    return lax.dot_general(a, b, (((0,), (0,)), ((), ())), preferred_element_type=f32)


def _sqrt_nonneg(y):
    return y * lax.rsqrt(jnp.maximum(y, F32_TINY))


def _gelu_tanh(x):
    k = -2.0 * GELU_C * LOG2_E
    return x * (1.0 / (1.0 + jnp.exp2(x * (k + (k * GELU_CUBIC) * (x * x)))))


def _gla_log_decay(z):
    scale = 1.0 / GLA_TAU
    return (jnp.minimum(z, 0.0) * scale
            - jnp.log2(1.0 + jnp.exp2(jnp.abs(z) * (-LOG2_E))) * (scale / LOG2_E))


def _lru_gates(u_half, half, p):
    lo = half * MXU_DIM
    hi = lo + MXU_DIM
    gates = _dot(u_half.astype(bf16), p.w_gate[half])
    r = jax.nn.sigmoid(gates[:, :MXU_DIM] + p.gate_a_b[:, lo:hi])
    i = jax.nn.sigmoid(gates[:, MXU_DIM:] + p.gate_x_b[:, lo:hi])
    neg_log_a = r * (LRU_C * jax.nn.softplus(-p.lam[:, lo:hi]))
    a = jnp.exp2(neg_log_a * (-LOG2_E))
    gain_sq = jnp.tanh(neg_log_a) * (a * a + 1.0)
    return a, _sqrt_nonneg(gain_sq) * (i * u_half)


def _scan_groups(a, x):
    rows, width = a.shape
    groups = rows // SUBLANES
    a3 = a.reshape(groups, SUBLANES, width)
    x3 = x.reshape(groups, SUBLANES, width)
    sub = lax.broadcasted_iota(jnp.int32, (groups, SUBLANES, width), 1)
    shift = 1
    while shift < SUBLANES:
        keep = sub >= shift
        a_prev = jnp.where(keep, pltpu.roll(a3, shift, 1), 1.0)
        x_prev = jnp.where(keep, pltpu.roll(x3, shift, 1), 0.0)
        x3 = x3 + a3 * x_prev
        a3 = a3 * a_prev
        shift *= 2
    return a3.reshape(rows, width), x3.reshape(rows, width)


def _lru_scan(a, x, h_in):
    rows = a.shape[0]
    big_a, big_x = _scan_groups(a, x)
    carry = h_in
    out = []
    for g in range(rows // SUBLANES):
        sl = slice(g * SUBLANES, (g + 1) * SUBLANES)
        hg = big_a[sl] * carry + big_x[sl]
        carry = hg[SUBLANES - 1:SUBLANES]
        out.append(hg)
    return jnp.concatenate(out, axis=0), carry


def _chunk_cumsum(g, chunk):
    row = lax.broadcasted_iota(jnp.int32, g.shape, 0) % chunk
    shift = 1
    while shift < chunk:
        g = g + jnp.where(row >= shift, pltpu.roll(g, shift, 0), 0.0)
        shift *= 2
    return g


def _chunk_cumsum_mxu(g, tri):
    hi = g.astype(bf16)
    lo = (g - hi.astype(f32)).astype(bf16)
    return _dot(tri, hi) + _dot(tri, lo)


def _head_stack(x, head_masks):
    zero = jnp.zeros_like(x)
    return jnp.concatenate([jnp.where(m, x, zero) for m in head_masks], axis=0)


def _gla_finish(o, g_gla, gn_g):
    heads = []
    for h in range(GLA_H):
        oh = o[:, h * GLA_DVH:(h + 1) * GLA_DVH]
        heads.append(oh * lax.rsqrt(jnp.mean(oh * oh, -1, keepdims=True) + RMS_EPS))
    return jnp.concatenate(heads, axis=1) * gn_g * jax.nn.silu(g_gla)


STEP_COST = dict(
    in_norm=0.6, in_lru=1.0, in_qk=0.5, in_v=0.5, in_g=0.6,
    conv=0.9, gates=0.7, scan=0.8, lru_finish=0.5,
    gla_prepare=0.9, gla_chunk=0.4, gla_finish=0.5,
    out_cols=0.5, out_norm=0.5,
    ffn_up=0.5, ffn_down=1.4, ffn_norm=0.5,
)


def _phase_project(t, x_of, p):
    def norm():
        t.h = _layer_norm(x_of(), p.ln_g[...], p.ln_b[...])
        t.hb = t.h.astype(bf16)

    def lru():
        t.p_lru = _dot(t.hb, p.w_in[:, COL_XLRU:COL_Q])

    def qk():
        t.p_qk = _dot(t.hb, p.w_in[:, COL_Q:COL_V])

    def v():
        t.p_v = _dot(t.hb, p.w_in[:, COL_V:COL_GGLA])

    def g():
        t.p_g = _dot(t.hb, p.w_in[:, COL_GGLA:IN_COLS_PAD])

    c = STEP_COST
    return [(norm, c["in_norm"]), (lru, c["in_lru"]), (qk, c["in_qk"]), (v, c["in_v"]), (g, c["in_g"])]


def _phase_lru(t, p, xl_ref, hcar_ref, want_out=True):
    halves = range(LRU_W // MXU_DIM)
    t.hs, t.carry, t.gates = {}, {}, {}

    def conv():
        x_lru = t.p_lru[:, :LRU_W]
        rows = x_lru.shape[0]
        xl_ref[SUBLANES:SUBLANES + rows, :] = x_lru
        cw = p.conv_w[...]
        u = p.conv_b[...] + cw[CONV_K - 1:CONV_K] * x_lru
        for j in range(CONV_K - 1):
            start = SUBLANES - (CONV_K - 1) + j
            u = u + cw[j:j + 1] * xl_ref[start:start + rows, :]
        xl_ref[0:SUBLANES, :] = xl_ref[rows:rows + SUBLANES, :]
        t.u = u
        t.h_in = hcar_ref[...]

    def gates(half):
        def step():
            t.gates[half] = _lru_gates(t.u[:, half * MXU_DIM:(half + 1) * MXU_DIM], half, p)
        return step

    def scan(half):
        def step():
            a, xin = t.gates[half]
            t.hs[half], t.carry[half] = _lru_scan(a, xin, t.h_in[:, half * MXU_DIM:(half + 1) * MXU_DIM])
        return step

    def finish():
        hcar_ref[...] = jnp.concatenate([t.carry[h] for h in halves], axis=1)
        if want_out:
            hs = jnp.concatenate([t.hs[h] for h in halves], axis=1)
            t.y_lru = (hs * _gelu_tanh(t.p_lru[:, LRU_W:])).astype(bf16)

    steps = [(conv, STEP_COST["conv"])]
    for half in halves:
        steps += [(gates(half), STEP_COST["gates"]), (scan(half), STEP_COST["scan"])]
    return steps + [(finish, STEP_COST["lru_finish"])]


def _phase_gla(t, p, chunk, scat_ref, tri=None, want_out=True):
    outs = []

    def prepare():
        q, k = t.p_qk[:, :GLA_DK], t.p_qk[:, GLA_DK:]
        n_chunks = q.shape[0] // chunk
        z = _dot(t.p_g[:, GLA_DV:].astype(bf16), p.al_w[...]) + p.al_b[...]
        g = _gla_log_decay(z)
        b = _chunk_cumsum(g, chunk) if tri is None else _chunk_cumsum_mxu(g, tri)
        last_rows = [b[(c + 1) * chunk - 1:(c + 1) * chunk] for c in range(n_chunks)]
        b_last = jnp.concatenate([jnp.broadcast_to(r, (chunk, GLA_DK)) for r in last_rows], axis=0)
        qs = q * (GLA_DKH ** -0.5)
        t.q_state = (qs * jnp.exp(b)).astype(bf16)
        t.k_end = (k * jnp.exp(b_last - b)).astype(bf16)
        t.q_end = (qs * jnp.exp(b - b_last)).astype(bf16)
        t.vb = t.p_v.astype(bf16)
        pad = jnp.zeros((SUBLANES - n_chunks, GLA_DK), f32)
        t.chunk_decay = jnp.exp(jnp.concatenate(last_rows + [pad], axis=0)).T
        lane = lax.broadcasted_iota(jnp.int32, (chunk, GLA_DK), 1)
        t.head_masks = [(lane // GLA_DKH) == h for h in range(GLA_H)]
        t_idx = lax.broadcasted_iota(jnp.int32, (GLA_H * chunk, chunk), 0) % chunk
        s_idx = lax.broadcasted_iota(jnp.int32, (GLA_H * chunk, chunk), 1)
        t.causal = s_idx <= t_idx

    def one_chunk(c):
        def step():
            sl = slice(c * chunk, (c + 1) * chunk)
            scores = _dot_nt(_head_stack(t.q_end[sl], t.head_masks), t.k_end[sl])
            probs = jnp.where(t.causal, scores, 0.0).astype(bf16)
            state = scat_ref[...]
            o_state = _dot(_head_stack(t.q_state[sl], t.head_masks), state.astype(bf16))
            v_heads = [t.vb[sl, h * GLA_DVH:(h + 1) * GLA_DVH] for h in range(GLA_H)]
            outs.append(jnp.concatenate(
                [o_state[h * chunk:(h + 1) * chunk] + _dot(probs[h * chunk:(h + 1) * chunk], v_heads[h])
                 for h in range(GLA_H)], axis=1))
            d_state = _dot_tn(_head_stack(t.k_end[sl], t.head_masks), jnp.concatenate(v_heads, axis=0))
            scat_ref[...] = t.chunk_decay[:, c:c + 1] * state + d_state
        return step

    def finish():
        if want_out:
            t.y_gla = _gla_finish(jnp.concatenate(outs, axis=0), t.p_g[:, :GLA_DV], p.gn_g[...]).astype(bf16)

    return ([(prepare, STEP_COST["gla_prepare"])]
            + [(one_chunk(c), STEP_COST["gla_chunk"]) for c in range(t.rows // chunk)]
            + [(finish, STEP_COST["gla_finish"])])


def _phase_out(t, p):
    mix = []

    def project(j):
        def step():
            cols = slice(j * WIDE_STEP_COLS, (j + 1) * WIDE_STEP_COLS)
            mix.append(_dot(t.y_lru, p.w_out[0:LRU_W, cols]) + _dot(t.y_gla, p.w_out[LRU_W:, cols]))
        return step

    def norm():
        t.h1 = _layer_norm(DEEPNORM_ALPHA * t.h + jnp.concatenate(mix, axis=1), p.ln1_g[...], p.ln1_b[...])
        t.h1b = t.h1.astype(bf16)

    return ([(project(j), STEP_COST["out_cols"]) for j in range(D_MODEL // WIDE_STEP_COLS)]
            + [(norm, STEP_COST["out_norm"])])


def _phase_ffn(t, fp, store):
    acts, outs = [], []

    def up(n):
        def step():
            cols = slice(n * MXU_DIM, (n + 1) * MXU_DIM)
            gate = _dot(t.h1b, fp.w_gate[:, cols])
            acts.append((jax.nn.silu(gate) * _dot(t.h1b, fp.w_up[:, cols])).astype(bf16))
        return step

    def down(j):
        def step():
            if j == 0:
                t.act = jnp.concatenate(acts, axis=1)
            outs.append(_dot(t.act, fp.w_down[:, j * WIDE_STEP_COLS:(j + 1) * WIDE_STEP_COLS]))
        return step

    def norm():
        ffn = jnp.concatenate(outs, axis=1)
        store(_layer_norm(DEEPNORM_ALPHA * t.h1 + ffn, fp.ln2_g[...], fp.ln2_b[...]))

    return ([(up(n), STEP_COST["ffn_up"]) for n in range(D_FF // MXU_DIM)]
            + [(down(j), STEP_COST["ffn_down"]) for j in range(D_MODEL // WIDE_STEP_COLS)]
            + [(norm, STEP_COST["ffn_norm"])])


def _run_interleaved(*phases):
    keyed = []
    for n, steps in enumerate(phases):
        total = sum(cost for _, cost in steps)
        done = 0.0
        for step, cost in steps:
            keyed.append(((done + 0.5 * cost) / total, n, step))
            done += cost
    for _, _, step in sorted(keyed, key=lambda e: e[:2]):
        step()


def _meta_kernel(meta_ref, *refs):
    p = MixerParams(*refs[:N_MIXER_PARAMS])
    xlc_ref, h_ref, s_ref, xl_scr = refs[N_MIXER_PARAMS:]
    xl_scr[...] = jnp.zeros_like(xl_scr)
    h_ref[...] = jnp.zeros_like(h_ref)
    s_ref[...] = jnp.zeros_like(s_ref)
    t = types.SimpleNamespace(rows=N_META)
    _run_interleaved(_phase_project(t, lambda: meta_ref[...], p))
    _run_interleaved(_phase_lru(t, p, xl_scr, h_ref, want_out=False))
    _run_interleaved(_phase_gla(t, p, N_META, s_ref, want_out=False))
    xlc_ref[...] = xl_scr[0:SUBLANES, :]


def _prompt_kernel(x_ref, xlc0_ref, h0_ref, s0_ref, tri_ref, *refs):
    p = MixerParams(*refs[:N_MIXER_PARAMS])
    fp = FfnParams(*refs[N_MIXER_PARAMS:N_MIXER_PARAMS + N_FFN_PARAMS])
    y_ref, gla_ref, lru_ref, conv_ref, xl_scr, hcar_scr, scat_scr = refs[N_MIXER_PARAMS + N_FFN_PARAMS:]
    step = pl.program_id(1)

    @pl.when(step == 0)
    def _():
        xl_scr[0:SUBLANES, :] = xlc0_ref[...]
        hcar_scr[...] = h0_ref[...]
        scat_scr[...] = s0_ref[...]

    n_sub = PROMPT_TILE // PROMPT_SUBTILE
    tiles = [types.SimpleNamespace(rows=PROMPT_SUBTILE) for _ in range(n_sub)]
    tri = tri_ref[...]

    def rows(s):
        return slice(s * PROMPT_SUBTILE, (s + 1) * PROMPT_SUBTILE)

    def project(s):
        return _phase_project(tiles[s], lambda: x_ref[0, rows(s), :], p)

    def ffn(s):
        def store(v):
            y_ref[0, rows(s), :] = v
        return _phase_ffn(tiles[s], fp, store)

    n_up = D_FF // MXU_DIM

    def after_mixers(s, ffn_rest):
        out_steps, ffn_steps = _phase_out(tiles[s], p), ffn(s)
        half = len(ffn_rest) // 2
        return (out_steps[:-1] + ffn_rest[:half] + out_steps[-1:] + ffn_rest[half:] + ffn_steps[:n_up],
                ffn_steps[n_up:])

    first = project(0)
    lru_ready = 2
    _run_interleaved(first[:lru_ready])
    pending, ffn_rest = [], []
    for s in range(n_sub):
        mixers = (_phase_lru(tiles[s], p, xl_scr, hcar_scr)
                  + _phase_gla(tiles[s], p, GLA_CHUNK, scat_scr, tri))
        others = (first[lru_ready:] if s == 0 else []) + (project(s + 1) if s + 1 < n_sub else [])
        _run_interleaved(mixers, *([others] if others else []), *([pending] if pending else []))
        pending, ffn_rest = after_mixers(s, ffn_rest)
    _run_interleaved(pending + ffn_rest)

    @pl.when(step == pl.num_programs(1) - 1)
    def _():
        gla_ref[0] = scat_scr[...]
        lru_ref[0] = hcar_scr[...]
        conv_ref[0] = xl_scr[SUBLANES - (CONV_K - 1):SUBLANES, :]


def _sample_kernel(x_ref, conv_in_ref, lru_in_ref, s_in_ref, *refs):
    p = MixerParams(*refs[:N_MIXER_PARAMS])
    (h1_ref, s_out_ref, lru_out_ref, conv_out_ref,
     hln_scr, ylru_scr, q_scr, k_scr, eg_scr, v_scr, gg_scr, o_scr) = refs[N_MIXER_PARAMS:]
    i = pl.program_id(0)

    @pl.when(i == 0)
    def _():
        h = _layer_norm(x_ref[...], p.ln_g[...], p.ln_b[...])
        hln_scr[...] = h
        proj = _dot(h.astype(bf16), p.w_in[...])
        x_lru = proj[:, COL_XLRU:COL_GLRU]
        cw = p.conv_w[...]
        u = p.conv_b[...] + cw[3:4] * x_lru
        for j in range(CONV_K - 1):
            u = u + cw[j:j + 1] * conv_in_ref[:, j, :]
        for j in range(CONV_K - 2):
            conv_out_ref[:, j, :] = conv_in_ref[:, j + 1, :]
        conv_out_ref[:, CONV_K - 2, :] = x_lru
        h_in = lru_in_ref[...]
        halves = []
        for half in range(LRU_W // MXU_DIM):
            lo, hi = half * MXU_DIM, (half + 1) * MXU_DIM
            a, xin = _lru_gates(u[:, lo:hi], half, p)
            halves.append(a * h_in[:, lo:hi] + xin)
        h_new = jnp.concatenate(halves, axis=1)
        lru_out_ref[...] = h_new
        ylru_scr[...] = h_new * _gelu_tanh(proj[:, COL_GLRU:COL_Q])
        z = _dot(proj[:, COL_ALR:COL_ALR + ALR_PAD].astype(bf16), p.al_w[...]) + p.al_b[...]
        eg_scr[...] = jnp.exp(_gla_log_decay(z))
        q_scr[...] = proj[:, COL_Q:COL_K] * (GLA_DKH ** -0.5)
        k_scr[...] = proj[:, COL_K:COL_V]
        v_scr[...] = proj[:, COL_V:COL_GGLA]
        gg_scr[...] = proj[:, COL_GGLA:COL_ALR]

    r0 = pl.multiple_of(i * SAMPLE_BLOCK, SAMPLE_BLOCK)
    q_t = q_scr[pl.ds(r0, SAMPLE_BLOCK), :].T
    k_t = k_scr[pl.ds(r0, SAMPLE_BLOCK), :].T
    e_t = eg_scr[pl.ds(r0, SAMPLE_BLOCK), :].T
    v_blk = v_scr[pl.ds(r0, SAMPLE_BLOCK), :]
    o_rows = []
    for j in range(SAMPLE_BLOCK):
        heads = []
        for h in range(GLA_H):
            ks = slice(h * GLA_DKH, (h + 1) * GLA_DKH)
            s_new = (e_t[ks, j:j + 1] * s_in_ref[j, h]
                     + k_t[ks, j:j + 1] * v_blk[j:j + 1, h * GLA_DVH:(h + 1) * GLA_DVH])
            s_out_ref[j, h] = s_new
            heads.append(jnp.sum(q_t[ks, j:j + 1] * s_new, axis=0, keepdims=True))
        o_rows.append(jnp.concatenate(heads, axis=1))
    o_scr[pl.ds(r0, SAMPLE_BLOCK), :] = jnp.concatenate(o_rows, axis=0)

    @pl.when(i == pl.num_programs(0) - 1)
    def _():
        y_gla = _gla_finish(o_scr[...], gg_scr[...], p.gn_g[...])
        y = jnp.concatenate([ylru_scr[...], y_gla], axis=1).astype(bf16)
        mix = _dot(y, p.w_out[...])
        h1_ref[...] = _layer_norm(DEEPNORM_ALPHA * hln_scr[...] + mix, p.ln1_g[...], p.ln1_b[...])


def _ffn_kernel(h_ref, *refs):
    fp = FfnParams(*refs[:N_FFN_PARAMS])
    out_ref, = refs[N_FFN_PARAMS:]
    h1 = h_ref[...]
    t = types.SimpleNamespace(h1=h1, h1b=h1.astype(bf16))

    def store(v):
        out_ref[...] = v

    _run_interleaved(_phase_ffn(t, fp, store))


def _block_diag(blocks):
    n, r, c = blocks.shape
    eye = jnp.eye(n, dtype=blocks.dtype)
    return (eye[:, None, :, None] * blocks[:, :, None, :]).reshape(n * r, n * c)


def _const_spec(arr):
    zeros = (0,) * arr.ndim
    return pl.BlockSpec(arr.shape, lambda *_: zeros, pipeline_mode=pl.Buffered(1))


def _mixer_params(ln_in_g, ln_in_b, w_in, conv_w, conv_b, ga_w, ga_b, gx_w, gx_b, lam, al_w, al_b, gn_g, w_out,
                  ln1_g, ln1_b):
    row = lambda a: a.reshape(1, -1).astype(f32)
    w_in_p = jnp.pad(w_in, ((0, 0), (0, IN_COLS_PAD - IN_COLS))).astype(bf16)
    per_half = MXU_DIM // LRU_BLK
    w_gate = jnp.stack([
        jnp.concatenate([_block_diag(ga_w[c * per_half:(c + 1) * per_half]),
                         _block_diag(gx_w[c * per_half:(c + 1) * per_half])], axis=1)
        for c in range(LRU_W // MXU_DIM)]).astype(bf16)
    al_w_p = jnp.pad(al_w, ((0, ALR_PAD - GLA_RANK), (0, 0))).astype(bf16)
    return MixerParams(row(ln_in_g), row(ln_in_b), w_in_p, conv_w.astype(f32), row(conv_b), w_gate, row(ga_b),
                       row(gx_b), row(lam), al_w_p, row(al_b), row(gn_g), w_out.astype(bf16), row(ln1_g), row(ln1_b))


def _meta_call(meta_tokens, params):
    out_shape = (jax.ShapeDtypeStruct((SUBLANES, LRU_W), f32),
                 jax.ShapeDtypeStruct((1, LRU_W), f32),
                 jax.ShapeDtypeStruct((GLA_DK, GLA_DVH), f32))
    return pl.pallas_call(
        _meta_kernel, out_shape=out_shape, name="meta_state",
        scratch_shapes=[pltpu.VMEM((N_META + SUBLANES, LRU_W), f32)],
        compiler_params=pltpu.CompilerParams(vmem_limit_bytes=VMEM_LIMIT_BYTES),
    )(meta_tokens, *params)


def _prompt_call(x, xlc0, h0, s0, params, ffn_params):
    batch, seq, _ = x.shape
    assert seq % PROMPT_TILE == 0 and PROMPT_TILE % PROMPT_SUBTILE == 0 and PROMPT_SUBTILE % GLA_CHUNK == 0
    n_tiles = seq // PROMPT_TILE
    pos = jnp.arange(PROMPT_SUBTILE)
    tri = ((pos[:, None] >= pos[None, :])
           & (pos[:, None] // GLA_CHUNK == pos[None, :] // GLA_CHUNK)).astype(bf16)
    seq_map = lambda b, t: (b, t, 0)
    state_map = lambda b, t: (b, 0, 0)
    out_shape = (jax.ShapeDtypeStruct((batch, seq, D_MODEL), f32),
                 jax.ShapeDtypeStruct((batch, GLA_DK, GLA_DVH), f32),
                 jax.ShapeDtypeStruct((batch, 1, LRU_W), f32),
                 jax.ShapeDtypeStruct((batch, CONV_K - 1, LRU_W), f32))
    return pl.pallas_call(
        _prompt_kernel, out_shape=out_shape, name="prompt_mixer",
        grid=(batch, n_tiles),
        in_specs=[pl.BlockSpec((1, PROMPT_TILE, D_MODEL), seq_map),
                  _const_spec(xlc0), _const_spec(h0), _const_spec(s0), _const_spec(tri)]
                 + [_const_spec(a) for a in params] + [_const_spec(a) for a in ffn_params],
        out_specs=(pl.BlockSpec((1, PROMPT_TILE, D_MODEL), seq_map),
                   pl.BlockSpec((1, GLA_DK, GLA_DVH), state_map),
                   pl.BlockSpec((1, 1, LRU_W), state_map),
                   pl.BlockSpec((1, CONV_K - 1, LRU_W), state_map)),
        scratch_shapes=[pltpu.VMEM((PROMPT_SUBTILE + SUBLANES, LRU_W), f32),
                        pltpu.VMEM((1, LRU_W), f32),
                        pltpu.VMEM((GLA_DK, GLA_DVH), f32)],
        compiler_params=pltpu.CompilerParams(dimension_semantics=("arbitrary", "arbitrary"),
                                             vmem_limit_bytes=VMEM_LIMIT_BYTES),
    )(x, xlc0, h0, s0, tri, *params, *ffn_params)


def _sample_call(x, conv_in, lru_in, gla_in, params):
    n = x.shape[0]
    assert n % SAMPLE_BLOCK == 0
    state_block = (SAMPLE_BLOCK, GLA_H, GLA_DKH, GLA_DVH)
    state_map = lambda i: (i, 0, 0, 0)
    out_shape = (jax.ShapeDtypeStruct((n, D_MODEL), f32),
                 jax.ShapeDtypeStruct(gla_in.shape, f32),
                 jax.ShapeDtypeStruct((n, LRU_W), f32),
                 jax.ShapeDtypeStruct((n, CONV_K - 1, LRU_W), f32))
    full = lambda shape: pl.BlockSpec(shape, lambda i: (0,) * len(shape))
    return pl.pallas_call(
        _sample_kernel, out_shape=out_shape, name="sample_mixer",
        grid=(n // SAMPLE_BLOCK,),
        in_specs=[_const_spec(x), _const_spec(conv_in), _const_spec(lru_in),
                  pl.BlockSpec(state_block, state_map)] + [_const_spec(a) for a in params],
        out_specs=(full((n, D_MODEL)), pl.BlockSpec(state_block, state_map), full((n, LRU_W)),
                   full((n, CONV_K - 1, LRU_W))),
        scratch_shapes=[pltpu.VMEM((n, D_MODEL), f32), pltpu.VMEM((n, LRU_W), f32),
                        pltpu.VMEM((n, GLA_DK), f32), pltpu.VMEM((n, GLA_DK), f32), pltpu.VMEM((n, GLA_DK), f32),
                        pltpu.VMEM((n, GLA_DV), f32), pltpu.VMEM((n, GLA_DV), f32), pltpu.VMEM((n, GLA_DV), f32)],
        compiler_params=pltpu.CompilerParams(dimension_semantics=("arbitrary",),
                                             vmem_limit_bytes=VMEM_LIMIT_BYTES),
    )(x, conv_in, lru_in, gla_in, *params)


def _ffn_call(h, ffn_params):
    return pl.pallas_call(
        _ffn_kernel, out_shape=jax.ShapeDtypeStruct(h.shape, f32), name="sample_ffn",
        compiler_params=pltpu.CompilerParams(vmem_limit_bytes=VMEM_LIMIT_BYTES),
    )(h, *ffn_params)


def kernel(x_prompt, x_sample, state_gla, state_lru, state_conv, meta_tokens, ln_in_g, ln_in_b, w_in, conv_w, conv_b, lru_gate_a_w, lru_gate_a_b, lru_gate_x_w, lru_gate_x_b, lru_lambda, gla_alpha_w, gla_alpha_b, gla_norm_g, w_out, ln1_g, ln1_b, w_ffn_gate, w_ffn_up, w_ffn_down, ln2_g, ln2_b):
    assert w_in.shape[0] == 1 and x_sample.shape[1] == 1, "one layer, one decode token per sample sequence"
    batch, seq, _ = x_prompt.shape
    n_sample = x_sample.shape[0]
    params = _mixer_params(ln_in_g, ln_in_b, w_in[0], conv_w[0], conv_b[0], lru_gate_a_w[0], lru_gate_a_b[0],
                           lru_gate_x_w[0], lru_gate_x_b[0], lru_lambda[0], gla_alpha_w[0], gla_alpha_b[0],
                           gla_norm_g[0], w_out[0], ln1_g[0], ln1_b[0])
    ffn_params = FfnParams(w_ffn_gate[0].astype(bf16), w_ffn_up[0].astype(bf16), w_ffn_down[0].astype(bf16),
                           ln2_g[0].reshape(1, -1), ln2_b[0].reshape(1, -1))

    xlc0, h0, s0 = _meta_call(meta_tokens, params)
    y_p, gla_p, lru_p, conv_p = _prompt_call(x_prompt, xlc0, h0, s0, params, ffn_params)
    h1_s, gla_s, lru_s, conv_s = _sample_call(x_sample[:, 0, :], state_conv[0], state_lru[0], state_gla[0], params)
    y_s = _ffn_call(h1_s, ffn_params)
    return (y_p,
            y_s.reshape(n_sample, 1, D_MODEL),
            gla_p.reshape(1, batch, GLA_H, GLA_DKH, GLA_DVH),
            lru_p.reshape(1, batch, LRU_W),
            conv_p[None],
            gla_s[None],
            lru_s[None],
            conv_s[None])
```

```python
import math
import types
from typing import NamedTuple

import jax
import jax.numpy as jnp
from jax import lax
from jax.experimental import pallas as pl
from jax.experimental.pallas import tpu as pltpu

f32 = jnp.float32
bf16 = jnp.bfloat16

D_MODEL = 1024
N_META = 16
LRU_W = 512
LRU_BLOCKS = 8
LRU_BLK = LRU_W // LRU_BLOCKS
LRU_C = 8.0
CONV_K = 4
GLA_H = 4
GLA_DKH = 64
GLA_DVH = 128
GLA_DK = GLA_H * GLA_DKH
GLA_DV = GLA_H * GLA_DVH
GLA_RANK = 16
GLA_TAU = 16.0
D_FF = 2816
DEEPNORM_ALPHA = 2.0 ** 0.25
LN_EPS = 1e-5
RMS_EPS = 1e-6
GELU_C = math.sqrt(2.0 / math.pi)
GELU_CUBIC = 0.044715
LOG2_E = math.log2(math.e)
F32_TINY = float(jnp.finfo(jnp.float32).tiny)

COL_XLRU = 0
COL_GLRU = COL_XLRU + LRU_W
COL_Q = COL_GLRU + LRU_W
COL_K = COL_Q + GLA_DK
COL_V = COL_K + GLA_DK
COL_GGLA = COL_V + GLA_DV
COL_ALR = COL_GGLA + GLA_DV
IN_COLS = COL_ALR + GLA_RANK

LANES = 128
SUBLANES = 8
MXU_DIM = 256
WIDE_STEP_COLS = 2 * MXU_DIM
ALR_PAD = LANES
IN_COLS_PAD = COL_ALR + ALR_PAD
VMEM_LIMIT_BYTES = 56 * 1024 * 1024

PROMPT_TILE = 512
PROMPT_SUBTILE = 256
GLA_CHUNK = 64
GLA_SAFE_LOG_DECAY = 80.0
SAMPLE_BLOCK = 2 * SUBLANES


class MixerParams(NamedTuple):
    ln_g: jax.Array
    ln_b: jax.Array
    w_in: jax.Array
    conv_w: jax.Array
    conv_b: jax.Array
    w_gate: jax.Array
    gate_a_b: jax.Array
    gate_x_b: jax.Array
    lam: jax.Array
    al_w: jax.Array
    al_b: jax.Array
    gn_g: jax.Array
    w_out: jax.Array
    ln1_g: jax.Array
    ln1_b: jax.Array


N_MIXER_PARAMS = len(MixerParams._fields)


class FfnParams(NamedTuple):
    w_gate: jax.Array
    w_up: jax.Array
    w_down: jax.Array
    ln2_g: jax.Array
    ln2_b: jax.Array


N_FFN_PARAMS = len(FfnParams._fields)


def _layer_norm(x, g, b):
    mu = jnp.mean(x, -1, keepdims=True)
    xc = x - mu
    var = jnp.mean(xc * xc, -1, keepdims=True)
    return xc * lax.rsqrt(var + LN_EPS) * g + b


def _dot(a, b):
    return jnp.dot(a, b, preferred_element_type=f32)


def _dot_nt(a, b):
    return lax.dot_general(a, b, (((1,), (1,)), ((), ())), preferred_element_type=f32)


def _dot_tn(a, b):
    return lax.dot_general(a, b, (((0,), (0,)), ((), ())), preferred_element_type=f32)


def _sqrt_nonneg(y):
    return y * lax.rsqrt(jnp.maximum(y, F32_TINY))


def _gelu_tanh(x):
    k = -2.0 * GELU_C * LOG2_E
    return x * (1.0 / (1.0 + jnp.exp2(x * (k + (k * GELU_CUBIC) * (x * x)))))


def _gla_log_decay(z):
    scale = 1.0 / GLA_TAU
    return (jnp.minimum(z, 0.0) * scale
            - jnp.log2(1.0 + jnp.exp2(jnp.abs(z) * (-LOG2_E))) * (scale / LOG2_E))


def _lru_gates(u_half, half, p):
    lo = half * MXU_DIM
    hi = lo + MXU_DIM
    gates = _dot(u_half.astype(bf16), p.w_gate[half])
    r = jax.nn.sigmoid(gates[:, :MXU_DIM] + p.gate_a_b[:, lo:hi])
    i = jax.nn.sigmoid(gates[:, MXU_DIM:] + p.gate_x_b[:, lo:hi])
    neg_log_a = r * (LRU_C * jax.nn.softplus(-p.lam[:, lo:hi]))
    a = jnp.exp2(neg_log_a * (-LOG2_E))
    gain_sq = jnp.tanh(neg_log_a) * (a * a + 1.0)
    return a, _sqrt_nonneg(gain_sq) * (i * u_half)


def _scan_groups(a, x):
    rows, width = a.shape
    groups = rows // SUBLANES
    a3 = a.reshape(groups, SUBLANES, width)
    x3 = x.reshape(groups, SUBLANES, width)
    sub = lax.broadcasted_iota(jnp.int32, (groups, SUBLANES, width), 1)
    shift = 1
    while shift < SUBLANES:
        keep = sub >= shift
        a_prev = jnp.where(keep, pltpu.roll(a3, shift, 1), 1.0)
        x_prev = jnp.where(keep, pltpu.roll(x3, shift, 1), 0.0)
        x3 = x3 + a3 * x_prev
        a3 = a3 * a_prev
        shift *= 2
    return a3.reshape(rows, width), x3.reshape(rows, width)


def _lru_scan(a, x, h_in):
    rows = a.shape[0]
    big_a, big_x = _scan_groups(a, x)
    carry = h_in
    out = []
    for g in range(rows // SUBLANES):
        sl = slice(g * SUBLANES, (g + 1) * SUBLANES)
        hg = big_a[sl] * carry + big_x[sl]
        carry = hg[SUBLANES - 1:SUBLANES]
        out.append(hg)
    return jnp.concatenate(out, axis=0), carry


def _chunk_cumsum(g, chunk):
    row = lax.broadcasted_iota(jnp.int32, g.shape, 0) % chunk
    shift = 1
    while shift < chunk:
        g = g + jnp.where(row >= shift, pltpu.roll(g, shift, 0), 0.0)
        shift *= 2
    return g


def _chunk_cumsum_mxu(g, tri):
    hi = g.astype(bf16)
    lo = (g - hi.astype(f32)).astype(bf16)
    return _dot(tri, hi) + _dot(tri, lo)


def _head_stack(x, head_masks):
    zero = jnp.zeros_like(x)
    return jnp.concatenate([jnp.where(m, x, zero) for m in head_masks], axis=0)


def _gla_finish(o, g_gla, gn_g):
    heads = []
    for h in range(GLA_H):
        oh = o[:, h * GLA_DVH:(h + 1) * GLA_DVH]
        heads.append(oh * lax.rsqrt(jnp.mean(oh * oh, -1, keepdims=True) + RMS_EPS))
    return jnp.concatenate(heads, axis=1) * gn_g * jax.nn.silu(g_gla)


STEP_COST = dict(
    in_norm=0.6, in_lru=1.0, in_qk=0.5, in_v=0.5, in_g=0.6,
    conv=0.9, gates=0.7, scan=0.8, lru_finish=0.5,
    gla_prepare=0.9, gla_chunk=0.4, gla_finish=0.5,
    out_cols=0.5, out_norm=0.5,
    ffn_up=0.5, ffn_down=1.4, ffn_norm=0.5,
)


def _phase_project(t, x_of, p):
    def norm():
        t.h = _layer_norm(x_of(), p.ln_g[...], p.ln_b[...])
        t.hb = t.h.astype(bf16)

    def lru():
        t.p_lru = _dot(t.hb, p.w_in[:, COL_XLRU:COL_Q])

    def qk():
        t.p_qk = _dot(t.hb, p.w_in[:, COL_Q:COL_V])

    def v():
        t.p_v = _dot(t.hb, p.w_in[:, COL_V:COL_GGLA])

    def g():
        t.p_g = _dot(t.hb, p.w_in[:, COL_GGLA:IN_COLS_PAD])

    c = STEP_COST
    return [(norm, c["in_norm"]), (lru, c["in_lru"]), (qk, c["in_qk"]), (v, c["in_v"]), (g, c["in_g"])]


def _phase_lru(t, p, xl_ref, hcar_ref, want_out=True):
    halves = range(LRU_W // MXU_DIM)
    t.hs, t.carry, t.gates = {}, {}, {}

    def conv():
        x_lru = t.p_lru[:, :LRU_W]
        rows = x_lru.shape[0]
        xl_ref[SUBLANES:SUBLANES + rows, :] = x_lru
        cw = p.conv_w[...]
        u = p.conv_b[...] + cw[CONV_K - 1:CONV_K] * x_lru
        for j in range(CONV_K - 1):
            start = SUBLANES - (CONV_K - 1) + j
            u = u + cw[j:j + 1] * xl_ref[start:start + rows, :]
        xl_ref[0:SUBLANES, :] = xl_ref[rows:rows + SUBLANES, :]
        t.u = u
        t.h_in = hcar_ref[...]

    def gates(half):
        def step():
            t.gates[half] = _lru_gates(t.u[:, half * MXU_DIM:(half + 1) * MXU_DIM], half, p)
        return step

    def scan(half):
        def step():
            a, xin = t.gates[half]
            t.hs[half], t.carry[half] = _lru_scan(a, xin, t.h_in[:, half * MXU_DIM:(half + 1) * MXU_DIM])
        return step

    def finish():
        hcar_ref[...] = jnp.concatenate([t.carry[h] for h in halves], axis=1)
        if want_out:
            hs = jnp.concatenate([t.hs[h] for h in halves], axis=1)
            t.y_lru = (hs * _gelu_tanh(t.p_lru[:, LRU_W:])).astype(bf16)

    steps = [(conv, STEP_COST["conv"])]
    for half in halves:
        steps += [(gates(half), STEP_COST["gates"]), (scan(half), STEP_COST["scan"])]
    return steps + [(finish, STEP_COST["lru_finish"])]


def _safe_scores(qs, k, b, head_masks):
    chunk = qs.shape[0]
    t_idx = lax.broadcasted_iota(jnp.int32, (GLA_H * chunk, chunk), 0) % chunk
    s_idx = lax.broadcasted_iota(jnp.int32, (GLA_H * chunk, chunk), 1)
    row = lax.broadcasted_iota(jnp.int32, (chunk, chunk), 0)
    col = lax.broadcasted_iota(jnp.int32, (chunk, chunk), 1)
    row_wide = lax.broadcasted_iota(jnp.int32, (chunk, GLA_DK), 0)
    diag = _dot_nt(_head_stack(qs.astype(bf16), head_masks), k.astype(bf16))
    total = jnp.where(t_idx == s_idx, diag, 0.0)
    size = 2
    while size <= chunk:
        half = size // 2
        split_row = (row // size) * size + (half - 1)
        pick = (col == split_row).astype(f32)
        b_split = jnp.dot(pick, b, preferred_element_type=f32, precision=lax.Precision.HIGHEST)
        decay = jnp.exp(-jnp.abs(b - b_split))
        second = (row_wide % size) >= half
        q_part = jnp.where(second, qs * decay, 0.0).astype(bf16)
        k_part = jnp.where(second, 0.0, k * decay).astype(bf16)
        part = _dot_nt(_head_stack(q_part, head_masks), k_part)
        total = total + jnp.where((t_idx // size) == (s_idx // size), part, 0.0)
        size *= 2
    return total


def _phase_gla(t, p, chunk, scat_ref, tri=None, want_out=True, safe=False, unsafe_ref=None):
    outs = []

    def prepare():
        q, k = t.p_qk[:, :GLA_DK], t.p_qk[:, GLA_DK:]
        n_chunks = q.shape[0] // chunk
        z = _dot(t.p_g[:, GLA_DV:].astype(bf16), p.al_w[...]) + p.al_b[...]
        g = _gla_log_decay(z)
        b = _chunk_cumsum(g, chunk) if tri is None else _chunk_cumsum_mxu(g, tri)
        last_rows = [b[(c + 1) * chunk - 1:(c + 1) * chunk] for c in range(n_chunks)]
        b_last = jnp.concatenate([jnp.broadcast_to(r, (chunk, GLA_DK)) for r in last_rows], axis=0)
        qs = q * (GLA_DKH ** -0.5)
        t.q_state = (qs * jnp.exp(b)).astype(bf16)
        t.k_end = (k * jnp.exp(b_last - b)).astype(bf16)
        if safe:
            t.qs, t.k, t.b = qs, k, b
        else:
            t.q_end = (qs * jnp.exp(b - b_last)).astype(bf16)
            if unsafe_ref is not None:
                strongest = jnp.min(jnp.concatenate(last_rows, axis=0))
                unsafe_ref[0] = jnp.where(strongest < -GLA_SAFE_LOG_DECAY, 1, unsafe_ref[0])
        t.vb = t.p_v.astype(bf16)
        pad = jnp.zeros((SUBLANES - n_chunks, GLA_DK), f32)
        t.chunk_decay = jnp.exp(jnp.concatenate(last_rows + [pad], axis=0)).T
        lane = lax.broadcasted_iota(jnp.int32, (chunk, GLA_DK), 1)
        t.head_masks = [(lane // GLA_DKH) == h for h in range(GLA_H)]
        t_idx = lax.broadcasted_iota(jnp.int32, (GLA_H * chunk, chunk), 0) % chunk
        s_idx = lax.broadcasted_iota(jnp.int32, (GLA_H * chunk, chunk), 1)
        t.causal = s_idx <= t_idx

    def one_chunk(c):
        def step():
            sl = slice(c * chunk, (c + 1) * chunk)
            if safe:
                probs = _safe_scores(t.qs[sl], t.k[sl], t.b[sl], t.head_masks).astype(bf16)
            else:
                scores = _dot_nt(_head_stack(t.q_end[sl], t.head_masks), t.k_end[sl])
                probs = jnp.where(t.causal, scores, 0.0).astype(bf16)
            state = scat_ref[...]
            o_state = _dot(_head_stack(t.q_state[sl], t.head_masks), state.astype(bf16))
            v_heads = [t.vb[sl, h * GLA_DVH:(h + 1) * GLA_DVH] for h in range(GLA_H)]
            outs.append(jnp.concatenate(
                [o_state[h * chunk:(h + 1) * chunk] + _dot(probs[h * chunk:(h + 1) * chunk], v_heads[h])
                 for h in range(GLA_H)], axis=1))
            d_state = _dot_tn(_head_stack(t.k_end[sl], t.head_masks), jnp.concatenate(v_heads, axis=0))
            scat_ref[...] = t.chunk_decay[:, c:c + 1] * state + d_state
        return step

    def finish():
        if want_out:
            t.y_gla = _gla_finish(jnp.concatenate(outs, axis=0), t.p_g[:, :GLA_DV], p.gn_g[...]).astype(bf16)

    return ([(prepare, STEP_COST["gla_prepare"])]
            + [(one_chunk(c), STEP_COST["gla_chunk"]) for c in range(t.rows // chunk)]
            + [(finish, STEP_COST["gla_finish"])])


def _phase_out(t, p):
    mix = []

    def project(j):
        def step():
            cols = slice(j * WIDE_STEP_COLS, (j + 1) * WIDE_STEP_COLS)
            mix.append(_dot(t.y_lru, p.w_out[0:LRU_W, cols]) + _dot(t.y_gla, p.w_out[LRU_W:, cols]))
        return step

    def norm():
        t.h1 = _layer_norm(DEEPNORM_ALPHA * t.h + jnp.concatenate(mix, axis=1), p.ln1_g[...], p.ln1_b[...])
        t.h1b = t.h1.astype(bf16)

    return ([(project(j), STEP_COST["out_cols"]) for j in range(D_MODEL // WIDE_STEP_COLS)]
            + [(norm, STEP_COST["out_norm"])])


def _phase_ffn(t, fp, store):
    acts, outs = [], []

    def up(n):
        def step():
            cols = slice(n * MXU_DIM, (n + 1) * MXU_DIM)
            gate = _dot(t.h1b, fp.w_gate[:, cols])
            acts.append((jax.nn.silu(gate) * _dot(t.h1b, fp.w_up[:, cols])).astype(bf16))
        return step

    def down(j):
        def step():
            if j == 0:
                t.act = jnp.concatenate(acts, axis=1)
            outs.append(_dot(t.act, fp.w_down[:, j * WIDE_STEP_COLS:(j + 1) * WIDE_STEP_COLS]))
        return step

    def norm():
        ffn = jnp.concatenate(outs, axis=1)
        store(_layer_norm(DEEPNORM_ALPHA * t.h1 + ffn, fp.ln2_g[...], fp.ln2_b[...]))

    return ([(up(n), STEP_COST["ffn_up"]) for n in range(D_FF // MXU_DIM)]
            + [(down(j), STEP_COST["ffn_down"]) for j in range(D_MODEL // WIDE_STEP_COLS)]
            + [(norm, STEP_COST["ffn_norm"])])


def _run_interleaved(*phases):
    keyed = []
    for n, steps in enumerate(phases):
        total = sum(cost for _, cost in steps)
        done = 0.0
        for step, cost in steps:
            keyed.append(((done + 0.5 * cost) / total, n, step))
            done += cost
    for _, _, step in sorted(keyed, key=lambda e: e[:2]):
        step()


def _meta_kernel(meta_ref, *refs):
    p = MixerParams(*refs[:N_MIXER_PARAMS])
    xlc_ref, h_ref, s_ref, xl_scr = refs[N_MIXER_PARAMS:]
    xl_scr[...] = jnp.zeros_like(xl_scr)
    h_ref[...] = jnp.zeros_like(h_ref)
    s_ref[...] = jnp.zeros_like(s_ref)
    t = types.SimpleNamespace(rows=N_META)
    _run_interleaved(_phase_project(t, lambda: meta_ref[...], p))
    _run_interleaved(_phase_lru(t, p, xl_scr, h_ref, want_out=False))
    _run_interleaved(_phase_gla(t, p, N_META, s_ref, want_out=False, safe=True))
    xlc_ref[...] = xl_scr[0:SUBLANES, :]


def _prompt_kernel(x_ref, xlc0_ref, h0_ref, s0_ref, tri_ref, *refs):
    p = MixerParams(*refs[:N_MIXER_PARAMS])
    fp = FfnParams(*refs[N_MIXER_PARAMS:N_MIXER_PARAMS + N_FFN_PARAMS])
    (y_ref, gla_ref, lru_ref, conv_ref, xl_scr, hcar_scr, scat_scr,
     xl_keep, hcar_keep, scat_keep, unsafe_ref) = refs[N_MIXER_PARAMS + N_FFN_PARAMS:]
    step = pl.program_id(1)

    @pl.when(step == 0)
    def _():
        xl_scr[0:SUBLANES, :] = xlc0_ref[...]
        hcar_scr[...] = h0_ref[...]
        scat_scr[...] = s0_ref[...]

    xl_keep[...] = xl_scr[0:SUBLANES, :]
    hcar_keep[...] = hcar_scr[...]
    scat_keep[...] = scat_scr[...]
    unsafe_ref[0] = 0

    n_sub = PROMPT_TILE // PROMPT_SUBTILE
    tiles = [types.SimpleNamespace(rows=PROMPT_SUBTILE) for _ in range(n_sub)]
    tri = tri_ref[...]

    def rows(s):
        return slice(s * PROMPT_SUBTILE, (s + 1) * PROMPT_SUBTILE)

    def project(s):
        return _phase_project(tiles[s], lambda: x_ref[0, rows(s), :], p)

    def ffn(s):
        def store(v):
            y_ref[0, rows(s), :] = v
        return _phase_ffn(tiles[s], fp, store)

    n_up = D_FF // MXU_DIM

    def after_mixers(s, ffn_rest):
        out_steps, ffn_steps = _phase_out(tiles[s], p), ffn(s)
        half = len(ffn_rest) // 2
        return (out_steps[:-1] + ffn_rest[:half] + out_steps[-1:] + ffn_rest[half:] + ffn_steps[:n_up],
                ffn_steps[n_up:])

    first = project(0)
    lru_ready = 2
    _run_interleaved(first[:lru_ready])
    pending, ffn_rest = [], []
    for s in range(n_sub):
        mixers = (_phase_lru(tiles[s], p, xl_scr, hcar_scr)
                  + _phase_gla(tiles[s], p, GLA_CHUNK, scat_scr, tri, unsafe_ref=unsafe_ref))
        others = (first[lru_ready:] if s == 0 else []) + (project(s + 1) if s + 1 < n_sub else [])
        _run_interleaved(mixers, *([others] if others else []), *([pending] if pending else []))
        pending, ffn_rest = after_mixers(s, ffn_rest)
    _run_interleaved(pending + ffn_rest)

    @pl.when(unsafe_ref[0] != 0)
    def _():
        xl_scr[0:SUBLANES, :] = xl_keep[...]
        hcar_scr[...] = hcar_keep[...]
        scat_scr[...] = scat_keep[...]

        def redo(s, carry):
            sub_rows = pl.ds(pl.multiple_of(s * PROMPT_SUBTILE, PROMPT_SUBTILE), PROMPT_SUBTILE)
            t = types.SimpleNamespace(rows=PROMPT_SUBTILE)

            def store(v):
                y_ref[0, sub_rows, :] = v

            _run_interleaved(_phase_project(t, lambda: x_ref[0, sub_rows, :], p)
                             + _phase_lru(t, p, xl_scr, hcar_scr)
                             + _phase_gla(t, p, GLA_CHUNK, scat_scr, tri, safe=True)
                             + _phase_out(t, p) + _phase_ffn(t, fp, store))
            return carry

        lax.fori_loop(0, n_sub, redo, 0)

    @pl.when(step == pl.num_programs(1) - 1)
    def _():
        gla_ref[0] = scat_scr[...]
        lru_ref[0] = hcar_scr[...]
        conv_ref[0] = xl_scr[SUBLANES - (CONV_K - 1):SUBLANES, :]


def _sample_kernel(x_ref, conv_in_ref, lru_in_ref, s_in_ref, *refs):
    p = MixerParams(*refs[:N_MIXER_PARAMS])
    (h1_ref, s_out_ref, lru_out_ref, conv_out_ref,
     hln_scr, ylru_scr, q_scr, k_scr, eg_scr, v_scr, gg_scr, o_scr) = refs[N_MIXER_PARAMS:]
    i = pl.program_id(0)

    @pl.when(i == 0)
    def _():
        h = _layer_norm(x_ref[...], p.ln_g[...], p.ln_b[...])
        hln_scr[...] = h
        proj = _dot(h.astype(bf16), p.w_in[...])
        x_lru = proj[:, COL_XLRU:COL_GLRU]
        cw = p.conv_w[...]
        u = p.conv_b[...] + cw[3:4] * x_lru
        for j in range(CONV_K - 1):
            u = u + cw[j:j + 1] * conv_in_ref[:, j, :]
        for j in range(CONV_K - 2):
            conv_out_ref[:, j, :] = conv_in_ref[:, j + 1, :]
        conv_out_ref[:, CONV_K - 2, :] = x_lru
        h_in = lru_in_ref[...]
        halves = []
        for half in range(LRU_W // MXU_DIM):
            lo, hi = half * MXU_DIM, (half + 1) * MXU_DIM
            a, xin = _lru_gates(u[:, lo:hi], half, p)
            halves.append(a * h_in[:, lo:hi] + xin)
        h_new = jnp.concatenate(halves, axis=1)
        lru_out_ref[...] = h_new
        ylru_scr[...] = h_new * _gelu_tanh(proj[:, COL_GLRU:COL_Q])
        z = _dot(proj[:, COL_ALR:COL_ALR + ALR_PAD].astype(bf16), p.al_w[...]) + p.al_b[...]
        eg_scr[...] = jnp.exp(_gla_log_decay(z))
        q_scr[...] = proj[:, COL_Q:COL_K] * (GLA_DKH ** -0.5)
        k_scr[...] = proj[:, COL_K:COL_V]
        v_scr[...] = proj[:, COL_V:COL_GGLA]
        gg_scr[...] = proj[:, COL_GGLA:COL_ALR]

    r0 = pl.multiple_of(i * SAMPLE_BLOCK, SAMPLE_BLOCK)
    q_t = q_scr[pl.ds(r0, SAMPLE_BLOCK), :].T
    k_t = k_scr[pl.ds(r0, SAMPLE_BLOCK), :].T
    e_t = eg_scr[pl.ds(r0, SAMPLE_BLOCK), :].T
    v_blk = v_scr[pl.ds(r0, SAMPLE_BLOCK), :]
    o_rows = []
    for j in range(SAMPLE_BLOCK):
        heads = []
        for h in range(GLA_H):
            ks = slice(h * GLA_DKH, (h + 1) * GLA_DKH)
            s_new = (e_t[ks, j:j + 1] * s_in_ref[j, h]
                     + k_t[ks, j:j + 1] * v_blk[j:j + 1, h * GLA_DVH:(h + 1) * GLA_DVH])
            s_out_ref[j, h] = s_new
            heads.append(jnp.sum(q_t[ks, j:j + 1] * s_new, axis=0, keepdims=True))
        o_rows.append(jnp.concatenate(heads, axis=1))
    o_scr[pl.ds(r0, SAMPLE_BLOCK), :] = jnp.concatenate(o_rows, axis=0)

    @pl.when(i == pl.num_programs(0) - 1)
    def _():
        y_gla = _gla_finish(o_scr[...], gg_scr[...], p.gn_g[...])
        y = jnp.concatenate([ylru_scr[...], y_gla], axis=1).astype(bf16)
        mix = _dot(y, p.w_out[...])
        h1_ref[...] = _layer_norm(DEEPNORM_ALPHA * hln_scr[...] + mix, p.ln1_g[...], p.ln1_b[...])


def _ffn_kernel(h_ref, *refs):
    fp = FfnParams(*refs[:N_FFN_PARAMS])
    out_ref, = refs[N_FFN_PARAMS:]
    h1 = h_ref[...]
    t = types.SimpleNamespace(h1=h1, h1b=h1.astype(bf16))

    def store(v):
        out_ref[...] = v

    _run_interleaved(_phase_ffn(t, fp, store))


def _block_diag(blocks):
    n, r, c = blocks.shape
    eye = jnp.eye(n, dtype=blocks.dtype)
    return (eye[:, None, :, None] * blocks[:, :, None, :]).reshape(n * r, n * c)


def _const_spec(arr):
    zeros = (0,) * arr.ndim
    return pl.BlockSpec(arr.shape, lambda *_: zeros, pipeline_mode=pl.Buffered(1))


def _mixer_params(ln_in_g, ln_in_b, w_in, conv_w, conv_b, ga_w, ga_b, gx_w, gx_b, lam, al_w, al_b, gn_g, w_out,
                  ln1_g, ln1_b):
    row = lambda a: a.reshape(1, -1).astype(f32)
    w_in_p = jnp.pad(w_in, ((0, 0), (0, IN_COLS_PAD - IN_COLS))).astype(bf16)
    per_half = MXU_DIM // LRU_BLK
    w_gate = jnp.stack([
        jnp.concatenate([_block_diag(ga_w[c * per_half:(c + 1) * per_half]),
                         _block_diag(gx_w[c * per_half:(c + 1) * per_half])], axis=1)
        for c in range(LRU_W // MXU_DIM)]).astype(bf16)
    al_w_p = jnp.pad(al_w, ((0, ALR_PAD - GLA_RANK), (0, 0))).astype(bf16)
    return MixerParams(row(ln_in_g), row(ln_in_b), w_in_p, conv_w.astype(f32), row(conv_b), w_gate, row(ga_b),
                       row(gx_b), row(lam), al_w_p, row(al_b), row(gn_g), w_out.astype(bf16), row(ln1_g), row(ln1_b))


def _meta_call(meta_tokens, params):
    out_shape = (jax.ShapeDtypeStruct((SUBLANES, LRU_W), f32),
                 jax.ShapeDtypeStruct((1, LRU_W), f32),
                 jax.ShapeDtypeStruct((GLA_DK, GLA_DVH), f32))
    return pl.pallas_call(
        _meta_kernel, out_shape=out_shape, name="meta_state",
        scratch_shapes=[pltpu.VMEM((N_META + SUBLANES, LRU_W), f32)],
        compiler_params=pltpu.CompilerParams(vmem_limit_bytes=VMEM_LIMIT_BYTES),
    )(meta_tokens, *params)


def _prompt_call(x, xlc0, h0, s0, params, ffn_params):
    batch, seq, _ = x.shape
    assert seq % PROMPT_TILE == 0 and PROMPT_TILE % PROMPT_SUBTILE == 0 and PROMPT_SUBTILE % GLA_CHUNK == 0
    n_tiles = seq // PROMPT_TILE
    pos = jnp.arange(PROMPT_SUBTILE)
    tri = ((pos[:, None] >= pos[None, :])
           & (pos[:, None] // GLA_CHUNK == pos[None, :] // GLA_CHUNK)).astype(bf16)
    seq_map = lambda b, t: (b, t, 0)
    state_map = lambda b, t: (b, 0, 0)
    out_shape = (jax.ShapeDtypeStruct((batch, seq, D_MODEL), f32),
                 jax.ShapeDtypeStruct((batch, GLA_DK, GLA_DVH), f32),
                 jax.ShapeDtypeStruct((batch, 1, LRU_W), f32),
                 jax.ShapeDtypeStruct((batch, CONV_K - 1, LRU_W), f32))
    return pl.pallas_call(
        _prompt_kernel, out_shape=out_shape, name="prompt_mixer",
        grid=(batch, n_tiles),
        in_specs=[pl.BlockSpec((1, PROMPT_TILE, D_MODEL), seq_map),
                  _const_spec(xlc0), _const_spec(h0), _const_spec(s0), _const_spec(tri)]
                 + [_const_spec(a) for a in params] + [_const_spec(a) for a in ffn_params],
        out_specs=(pl.BlockSpec((1, PROMPT_TILE, D_MODEL), seq_map),
                   pl.BlockSpec((1, GLA_DK, GLA_DVH), state_map),
                   pl.BlockSpec((1, 1, LRU_W), state_map),
                   pl.BlockSpec((1, CONV_K - 1, LRU_W), state_map)),
        scratch_shapes=[pltpu.VMEM((PROMPT_SUBTILE + SUBLANES, LRU_W), f32),
                        pltpu.VMEM((1, LRU_W), f32),
                        pltpu.VMEM((GLA_DK, GLA_DVH), f32),
                        pltpu.VMEM((SUBLANES, LRU_W), f32),
                        pltpu.VMEM((1, LRU_W), f32),
                        pltpu.VMEM((GLA_DK, GLA_DVH), f32),
                        pltpu.SMEM((1,), jnp.int32)],
        compiler_params=pltpu.CompilerParams(dimension_semantics=("arbitrary", "arbitrary"),
                                             vmem_limit_bytes=VMEM_LIMIT_BYTES),
    )(x, xlc0, h0, s0, tri, *params, *ffn_params)


def _sample_call(x, conv_in, lru_in, gla_in, params):
    n = x.shape[0]
    assert n % SAMPLE_BLOCK == 0
    state_block = (SAMPLE_BLOCK, GLA_H, GLA_DKH, GLA_DVH)
    state_map = lambda i: (i, 0, 0, 0)
    out_shape = (jax.ShapeDtypeStruct((n, D_MODEL), f32),
                 jax.ShapeDtypeStruct(gla_in.shape, f32),
                 jax.ShapeDtypeStruct((n, LRU_W), f32),
                 jax.ShapeDtypeStruct((n, CONV_K - 1, LRU_W), f32))
    full = lambda shape: pl.BlockSpec(shape, lambda i: (0,) * len(shape))
    return pl.pallas_call(
        _sample_kernel, out_shape=out_shape, name="sample_mixer",
        grid=(n // SAMPLE_BLOCK,),
        in_specs=[_const_spec(x), _const_spec(conv_in), _const_spec(lru_in),
                  pl.BlockSpec(state_block, state_map)] + [_const_spec(a) for a in params],
        out_specs=(full((n, D_MODEL)), pl.BlockSpec(state_block, state_map), full((n, LRU_W)),
                   full((n, CONV_K - 1, LRU_W))),
        scratch_shapes=[pltpu.VMEM((n, D_MODEL), f32), pltpu.VMEM((n, LRU_W), f32),
                        pltpu.VMEM((n, GLA_DK), f32), pltpu.VMEM((n, GLA_DK), f32), pltpu.VMEM((n, GLA_DK), f32),
                        pltpu.VMEM((n, GLA_DV), f32), pltpu.VMEM((n, GLA_DV), f32), pltpu.VMEM((n, GLA_DV), f32)],
        compiler_params=pltpu.CompilerParams(dimension_semantics=("arbitrary",),
                                             vmem_limit_bytes=VMEM_LIMIT_BYTES),
    )(x, conv_in, lru_in, gla_in, *params)


def _ffn_call(h, ffn_params):
    return pl.pallas_call(
        _ffn_kernel, out_shape=jax.ShapeDtypeStruct(h.shape, f32), name="sample_ffn",
        compiler_params=pltpu.CompilerParams(vmem_limit_bytes=VMEM_LIMIT_BYTES),
    )(h, *ffn_params)


def kernel(x_prompt, x_sample, state_gla, state_lru, state_conv, meta_tokens, ln_in_g, ln_in_b, w_in, conv_w, conv_b, lru_gate_a_w, lru_gate_a_b, lru_gate_x_w, lru_gate_x_b, lru_lambda, gla_alpha_w, gla_alpha_b, gla_norm_g, w_out, ln1_g, ln1_b, w_ffn_gate, w_ffn_up, w_ffn_down, ln2_g, ln2_b):
    assert w_in.shape[0] == 1 and x_sample.shape[1] == 1, "one layer, one decode token per sample sequence"
    batch, seq, _ = x_prompt.shape
    n_sample = x_sample.shape[0]
    params = _mixer_params(ln_in_g, ln_in_b, w_in[0], conv_w[0], conv_b[0], lru_gate_a_w[0], lru_gate_a_b[0],
                           lru_gate_x_w[0], lru_gate_x_b[0], lru_lambda[0], gla_alpha_w[0], gla_alpha_b[0],
                           gla_norm_g[0], w_out[0], ln1_g[0], ln1_b[0])
    ffn_params = FfnParams(w_ffn_gate[0].astype(bf16), w_ffn_up[0].astype(bf16), w_ffn_down[0].astype(bf16),
                           ln2_g[0].reshape(1, -1), ln2_b[0].reshape(1, -1))

    xlc0, h0, s0 = _meta_call(meta_tokens, params)
    y_p, gla_p, lru_p, conv_p = _prompt_call(x_prompt, xlc0, h0, s0, params, ffn_params)
    h1_s, gla_s, lru_s, conv_s = _sample_call(x_sample[:, 0, :], state_conv[0], state_lru[0], state_gla[0], params)
    y_s = _ffn_call(h1_s, ffn_params)
    return (y_p,
            y_s.reshape(n_sample, 1, D_MODEL),
            gla_p.reshape(1, batch, GLA_H, GLA_DKH, GLA_DVH),
            lru_p.reshape(1, batch, LRU_W),
            conv_p[None],
            gla_s[None],
            lru_s[None],
            conv_s[None])
```

```python
import math
import types
from typing import NamedTuple

import jax
import jax.numpy as jnp
from jax import lax
from jax.experimental import pallas as pl
from jax.experimental.pallas import tpu as pltpu

f32 = jnp.float32
bf16 = jnp.bfloat16

D_MODEL = 1024
N_META = 16
LRU_W = 512
LRU_BLOCKS = 8
LRU_BLK = LRU_W // LRU_BLOCKS
LRU_C = 8.0
CONV_K = 4
GLA_H = 4
GLA_DKH = 64
GLA_DVH = 128
GLA_DK = GLA_H * GLA_DKH
GLA_DV = GLA_H * GLA_DVH
GLA_RANK = 16
GLA_TAU = 16.0
D_FF = 2816
DEEPNORM_ALPHA = 2.0 ** 0.25
LN_EPS = 1e-5
RMS_EPS = 1e-6
GELU_C = math.sqrt(2.0 / math.pi)
GELU_CUBIC = 0.044715
LOG2_E = math.log2(math.e)
F32_TINY = float(jnp.finfo(jnp.float32).tiny)

COL_XLRU = 0
COL_GLRU = COL_XLRU + LRU_W
COL_Q = COL_GLRU + LRU_W
COL_K = COL_Q + GLA_DK
COL_V = COL_K + GLA_DK
COL_GGLA = COL_V + GLA_DV
COL_ALR = COL_GGLA + GLA_DV
IN_COLS = COL_ALR + GLA_RANK

LANES = 128
SUBLANES = 8
MXU_DIM = 256
WIDE_STEP_COLS = 2 * MXU_DIM
ALR_PAD = LANES
IN_COLS_PAD = COL_ALR + ALR_PAD
VMEM_LIMIT_BYTES = 56 * 1024 * 1024

PROMPT_TILE = 512
PROMPT_SUBTILE = 256
GLA_CHUNK = 64
GLA_SAFE_LOG_DECAY = 80.0
SAMPLE_BLOCK = 2 * SUBLANES


class MixerParams(NamedTuple):
    ln_g: jax.Array
    ln_b: jax.Array
    w_in: jax.Array
    conv_w: jax.Array
    conv_b: jax.Array
    w_gate: jax.Array
    gate_a_b: jax.Array
    gate_x_b: jax.Array
    lam: jax.Array
    al_w: jax.Array
    al_b: jax.Array
    gn_g: jax.Array
    w_out: jax.Array
    ln1_g: jax.Array
    ln1_b: jax.Array


N_MIXER_PARAMS = len(MixerParams._fields)


class FfnParams(NamedTuple):
    w_gate: jax.Array
    w_up: jax.Array
    w_down: jax.Array
    ln2_g: jax.Array
    ln2_b: jax.Array


N_FFN_PARAMS = len(FfnParams._fields)


def _layer_norm(x, g, b):
    mu = jnp.mean(x, -1, keepdims=True)
    xc = x - mu
    var = jnp.mean(xc * xc, -1, keepdims=True)
    return xc * lax.rsqrt(var + LN_EPS) * g + b


def _dot(a, b):
    return jnp.dot(a, b, preferred_element_type=f32)


def _dot_nt(a, b):
    return lax.dot_general(a, b, (((1,), (1,)), ((), ())), preferred_element_type=f32)


def _dot_tn(a, b):
    return lax.dot_general(a, b, (((0,), (0,)), ((), ())), preferred_element_type=f32)


def _sqrt_nonneg(y):
    return y * lax.rsqrt(jnp.maximum(y, F32_TINY))


def _gelu_tanh(x):
    k = -2.0 * GELU_C * LOG2_E
    return x * (1.0 / (1.0 + jnp.exp2(x * (k + (k * GELU_CUBIC) * (x * x)))))


def _gla_log_decay(z):
    scale = 1.0 / GLA_TAU
    return (jnp.minimum(z, 0.0) * scale
            - jnp.log2(1.0 + jnp.exp2(jnp.abs(z) * (-LOG2_E))) * (scale / LOG2_E))


def _lru_gates(u_half, half, p):
    lo = half * MXU_DIM
    hi = lo + MXU_DIM
    gates = _dot(u_half.astype(bf16), p.w_gate[half])
    r = jax.nn.sigmoid(gates[:, :MXU_DIM] + p.gate_a_b[:, lo:hi])
    i = jax.nn.sigmoid(gates[:, MXU_DIM:] + p.gate_x_b[:, lo:hi])
    neg_log_a = r * (LRU_C * jax.nn.softplus(-p.lam[:, lo:hi]))
    a = jnp.exp2(neg_log_a * (-LOG2_E))
    gain_sq = jnp.tanh(neg_log_a) * (a * a + 1.0)
    return a, _sqrt_nonneg(gain_sq) * (i * u_half)


def _scan_groups(a, x):
    rows, width = a.shape
    groups = rows // SUBLANES
    a3 = a.reshape(groups, SUBLANES, width)
    x3 = x.reshape(groups, SUBLANES, width)
    sub = lax.broadcasted_iota(jnp.int32, (groups, SUBLANES, width), 1)
    shift = 1
    while shift < SUBLANES:
        keep = sub >= shift
        a_prev = jnp.where(keep, pltpu.roll(a3, shift, 1), 1.0)
        x_prev = jnp.where(keep, pltpu.roll(x3, shift, 1), 0.0)
        x3 = x3 + a3 * x_prev
        a3 = a3 * a_prev
        shift *= 2
    return a3.reshape(rows, width), x3.reshape(rows, width)


def _lru_scan(a, x, h_in):
    rows = a.shape[0]
    big_a, big_x = _scan_groups(a, x)
    carry = h_in
    out = []
    for g in range(rows // SUBLANES):
        sl = slice(g * SUBLANES, (g + 1) * SUBLANES)
        hg = big_a[sl] * carry + big_x[sl]
        carry = hg[SUBLANES - 1:SUBLANES]
        out.append(hg)
    return jnp.concatenate(out, axis=0), carry


def _chunk_cumsum(g, chunk):
    row = lax.broadcasted_iota(jnp.int32, g.shape, 0) % chunk
    shift = 1
    while shift < chunk:
        g = g + jnp.where(row >= shift, pltpu.roll(g, shift, 0), 0.0)
        shift *= 2
    return g


def _chunk_cumsum_mxu(g, tri):
    hi = g.astype(bf16)
    lo = (g - hi.astype(f32)).astype(bf16)
    return _dot(tri, hi) + _dot(tri, lo)


def _head_stack(x, head_masks):
    zero = jnp.zeros_like(x)
    return jnp.concatenate([jnp.where(m, x, zero) for m in head_masks], axis=0)


def _gla_finish(o, g_gla, gn_g):
    heads = []
    for h in range(GLA_H):
        oh = o[:, h * GLA_DVH:(h + 1) * GLA_DVH]
        heads.append(oh * lax.rsqrt(jnp.mean(oh * oh, -1, keepdims=True) + RMS_EPS))
    return jnp.concatenate(heads, axis=1) * gn_g * jax.nn.silu(g_gla)


STEP_COST = dict(
    in_norm=0.6, in_lru=1.0, in_qk=0.5, in_v=0.5, in_g=0.6,
    conv=0.9, gates=0.7, scan=0.8, lru_finish=0.5,
    gla_prepare=0.9, gla_chunk=0.4, gla_finish=0.5,
    out_cols=0.5, out_norm=0.5,
    ffn_up=0.5, ffn_down=1.4, ffn_norm=0.5,
)


def _phase_project(t, x_of, p):
    def norm():
        t.h = _layer_norm(x_of(), p.ln_g[...], p.ln_b[...])
        t.hb = t.h.astype(bf16)

    def lru():
        t.p_lru = _dot(t.hb, p.w_in[:, COL_XLRU:COL_Q])

    def qk():
        t.p_qk = _dot(t.hb, p.w_in[:, COL_Q:COL_V])

    def v():
        t.p_v = _dot(t.hb, p.w_in[:, COL_V:COL_GGLA])

    def g():
        t.p_g = _dot(t.hb, p.w_in[:, COL_GGLA:IN_COLS_PAD])

    c = STEP_COST
    return [(norm, c["in_norm"]), (lru, c["in_lru"]), (qk, c["in_qk"]), (v, c["in_v"]), (g, c["in_g"])]


def _phase_lru(t, p, xl_ref, hcar_ref, want_out=True):
    halves = range(LRU_W // MXU_DIM)
    t.hs, t.carry, t.gates = {}, {}, {}

    def conv():
        x_lru = t.p_lru[:, :LRU_W]
        rows = x_lru.shape[0]
        xl_ref[SUBLANES:SUBLANES + rows, :] = x_lru
        cw = p.conv_w[...]
        u = p.conv_b[...] + cw[CONV_K - 1:CONV_K] * x_lru
        for j in range(CONV_K - 1):
            start = SUBLANES - (CONV_K - 1) + j
            u = u + cw[j:j + 1] * xl_ref[start:start + rows, :]
        xl_ref[0:SUBLANES, :] = xl_ref[rows:rows + SUBLANES, :]
        t.u = u
        t.h_in = hcar_ref[...]

    def gates(half):
        def step():
            t.gates[half] = _lru_gates(t.u[:, half * MXU_DIM:(half + 1) * MXU_DIM], half, p)
        return step

    def scan(half):
        def step():
            a, xin = t.gates[half]
            t.hs[half], t.carry[half] = _lru_scan(a, xin, t.h_in[:, half * MXU_DIM:(half + 1) * MXU_DIM])
        return step

    def finish():
        hcar_ref[...] = jnp.concatenate([t.carry[h] for h in halves], axis=1)
        if want_out:
            hs = jnp.concatenate([t.hs[h] for h in halves], axis=1)
            t.y_lru = (hs * _gelu_tanh(t.p_lru[:, LRU_W:])).astype(bf16)

    steps = [(conv, STEP_COST["conv"])]
    for half in halves:
        steps += [(gates(half), STEP_COST["gates"]), (scan(half), STEP_COST["scan"])]
    return steps + [(finish, STEP_COST["lru_finish"])]


def _safe_scores(qs, k, b, head_masks):
    chunk = qs.shape[0]
    t_idx = lax.broadcasted_iota(jnp.int32, (GLA_H * chunk, chunk), 0) % chunk
    s_idx = lax.broadcasted_iota(jnp.int32, (GLA_H * chunk, chunk), 1)
    row = lax.broadcasted_iota(jnp.int32, (chunk, chunk), 0)
    col = lax.broadcasted_iota(jnp.int32, (chunk, chunk), 1)
    row_wide = lax.broadcasted_iota(jnp.int32, (chunk, GLA_DK), 0)
    diag = _dot_nt(_head_stack(qs.astype(bf16), head_masks), k.astype(bf16))
    total = jnp.where(t_idx == s_idx, diag, 0.0)
    size = 2
    while size <= chunk:
        half = size // 2
        split_row = (row // size) * size + (half - 1)
        pick = (col == split_row).astype(f32)
        b_split = jnp.dot(pick, b, preferred_element_type=f32, precision=lax.Precision.HIGHEST)
        decay = jnp.exp(-jnp.abs(b - b_split))
        second = (row_wide % size) >= half
        q_part = jnp.where(second, qs * decay, 0.0).astype(bf16)
        k_part = jnp.where(second, 0.0, k * decay).astype(bf16)
        part = _dot_nt(_head_stack(q_part, head_masks), k_part)
        total = total + jnp.where((t_idx // size) == (s_idx // size), part, 0.0)
        size *= 2
    return total


def _phase_gla(t, p, chunk, scat_ref, tri=None, want_out=True, safe=False, unsafe_ref=None):
    outs = []

    def prepare():
        q, k = t.p_qk[:, :GLA_DK], t.p_qk[:, GLA_DK:]
        n_chunks = q.shape[0] // chunk
        z = _dot(t.p_g[:, GLA_DV:].astype(bf16), p.al_w[...]) + p.al_b[...]
        g = _gla_log_decay(z)
        b = _chunk_cumsum(g, chunk) if tri is None else _chunk_cumsum_mxu(g, tri)
        last_rows = [b[(c + 1) * chunk - 1:(c + 1) * chunk] for c in range(n_chunks)]
        b_last = jnp.concatenate([jnp.broadcast_to(r, (chunk, GLA_DK)) for r in last_rows], axis=0)
        qs = q * (GLA_DKH ** -0.5)
        t.q_state = (qs * jnp.exp(b)).astype(bf16)
        t.k_end = (k * jnp.exp(b_last - b)).astype(bf16)
        if safe:
            t.qs, t.k, t.b = qs, k, b
        else:
            t.q_end = (qs * jnp.exp(b - b_last)).astype(bf16)
            if unsafe_ref is not None:
                strongest = jnp.min(jnp.concatenate(last_rows, axis=0))
                unsafe_ref[0] = jnp.where(strongest < -GLA_SAFE_LOG_DECAY, 1, unsafe_ref[0])
        t.vb = t.p_v.astype(bf16)
        pad = jnp.zeros((SUBLANES - n_chunks, GLA_DK), f32)
        t.chunk_decay = jnp.exp(jnp.concatenate(last_rows + [pad], axis=0)).T
        lane = lax.broadcasted_iota(jnp.int32, (chunk, GLA_DK), 1)
        t.head_masks = [(lane // GLA_DKH) == h for h in range(GLA_H)]
        t_idx = lax.broadcasted_iota(jnp.int32, (GLA_H * chunk, chunk), 0) % chunk
        s_idx = lax.broadcasted_iota(jnp.int32, (GLA_H * chunk, chunk), 1)
        t.causal = s_idx <= t_idx

    def one_chunk(c):
        def step():
            sl = slice(c * chunk, (c + 1) * chunk)
            if safe:
                probs = _safe_scores(t.qs[sl], t.k[sl], t.b[sl], t.head_masks).astype(bf16)
            else:
                scores = _dot_nt(_head_stack(t.q_end[sl], t.head_masks), t.k_end[sl])
                probs = jnp.where(t.causal, scores, 0.0).astype(bf16)
            state = scat_ref[...]
            o_state = _dot(_head_stack(t.q_state[sl], t.head_masks), state.astype(bf16))
            v_heads = [t.vb[sl, h * GLA_DVH:(h + 1) * GLA_DVH] for h in range(GLA_H)]
            outs.append(jnp.concatenate(
                [o_state[h * chunk:(h + 1) * chunk] + _dot(probs[h * chunk:(h + 1) * chunk], v_heads[h])
                 for h in range(GLA_H)], axis=1))
            d_state = _dot_tn(_head_stack(t.k_end[sl], t.head_masks), jnp.concatenate(v_heads, axis=0))
            scat_ref[...] = t.chunk_decay[:, c:c + 1] * state + d_state
        return step

    def finish():
        if want_out:
            t.y_gla = _gla_finish(jnp.concatenate(outs, axis=0), t.p_g[:, :GLA_DV], p.gn_g[...]).astype(bf16)

    return ([(prepare, STEP_COST["gla_prepare"])]
            + [(one_chunk(c), STEP_COST["gla_chunk"]) for c in range(t.rows // chunk)]
            + [(finish, STEP_COST["gla_finish"])])


def _phase_out(t, p):
    mix = []

    def project(j):
        def step():
            cols = slice(j * WIDE_STEP_COLS, (j + 1) * WIDE_STEP_COLS)
            mix.append(_dot(t.y_lru, p.w_out[0:LRU_W, cols]) + _dot(t.y_gla, p.w_out[LRU_W:, cols]))
        return step

    def norm():
        t.h1 = _layer_norm(DEEPNORM_ALPHA * t.h + jnp.concatenate(mix, axis=1), p.ln1_g[...], p.ln1_b[...])
        t.h1b = t.h1.astype(bf16)

    return ([(project(j), STEP_COST["out_cols"]) for j in range(D_MODEL // WIDE_STEP_COLS)]
            + [(norm, STEP_COST["out_norm"])])


def _phase_ffn(t, fp, store):
    acts, outs = [], []

    def up(n):
        def step():
            cols = slice(n * MXU_DIM, (n + 1) * MXU_DIM)
            gate = _dot(t.h1b, fp.w_gate[:, cols])
            acts.append((jax.nn.silu(gate) * _dot(t.h1b, fp.w_up[:, cols])).astype(bf16))
        return step

    def down(j):
        def step():
            if j == 0:
                t.act = jnp.concatenate(acts, axis=1)
            outs.append(_dot(t.act, fp.w_down[:, j * WIDE_STEP_COLS:(j + 1) * WIDE_STEP_COLS]))
        return step

    def norm():
        ffn = jnp.concatenate(outs, axis=1)
        store(_layer_norm(DEEPNORM_ALPHA * t.h1 + ffn, fp.ln2_g[...], fp.ln2_b[...]))

    return ([(up(n), STEP_COST["ffn_up"]) for n in range(D_FF // MXU_DIM)]
            + [(down(j), STEP_COST["ffn_down"]) for j in range(D_MODEL // WIDE_STEP_COLS)]
            + [(norm, STEP_COST["ffn_norm"])])


def _run_interleaved(*phases):
    keyed = []
    for n, steps in enumerate(phases):
        total = sum(cost for _, cost in steps)
        done = 0.0
        for step, cost in steps:
            keyed.append(((done + 0.5 * cost) / total, n, step))
            done += cost
    for _, _, step in sorted(keyed, key=lambda e: e[:2]):
        step()


def _meta_kernel(meta_ref, *refs):
    p = MixerParams(*refs[:N_MIXER_PARAMS])
    xlc_ref, h_ref, s_ref, xl_scr = refs[N_MIXER_PARAMS:]
    xl_scr[...] = jnp.zeros_like(xl_scr)
    h_ref[...] = jnp.zeros_like(h_ref)
    s_ref[...] = jnp.zeros_like(s_ref)
    t = types.SimpleNamespace(rows=N_META)
    _run_interleaved(_phase_project(t, lambda: meta_ref[...], p))
    _run_interleaved(_phase_lru(t, p, xl_scr, h_ref, want_out=False))
    _run_interleaved(_phase_gla(t, p, N_META, s_ref, want_out=False, safe=True))
    xlc_ref[...] = xl_scr[0:SUBLANES, :]


def _prompt_kernel(x_ref, xlc0_ref, h0_ref, s0_ref, tri_ref, *refs):
    p = MixerParams(*refs[:N_MIXER_PARAMS])
    fp = FfnParams(*refs[N_MIXER_PARAMS:N_MIXER_PARAMS + N_FFN_PARAMS])
    (y_ref, gla_ref, lru_ref, conv_ref, xl_scr, hcar_scr, scat_scr,
     xl_keep, hcar_keep, scat_keep, unsafe_ref) = refs[N_MIXER_PARAMS + N_FFN_PARAMS:]
    step = pl.program_id(1)

    @pl.when(step == 0)
    def _():
        xl_scr[0:SUBLANES, :] = xlc0_ref[...]
        hcar_scr[...] = h0_ref[...]
        scat_scr[...] = s0_ref[...]

    xl_keep[...] = xl_scr[0:SUBLANES, :]
    hcar_keep[...] = hcar_scr[...]
    scat_keep[...] = scat_scr[...]
    unsafe_ref[0] = 0

    n_sub = PROMPT_TILE // PROMPT_SUBTILE
    tiles = [types.SimpleNamespace(rows=PROMPT_SUBTILE) for _ in range(n_sub)]
    tri = tri_ref[...]

    def rows(s):
        return slice(s * PROMPT_SUBTILE, (s + 1) * PROMPT_SUBTILE)

    def project(s):
        return _phase_project(tiles[s], lambda: x_ref[0, rows(s), :], p)

    def ffn(s):
        def store(v):
            y_ref[0, rows(s), :] = v
        return _phase_ffn(tiles[s], fp, store)

    n_up = D_FF // MXU_DIM

    def after_mixers(s, ffn_rest):
        out_steps, ffn_steps = _phase_out(tiles[s], p), ffn(s)
        half = len(ffn_rest) // 2
        return (out_steps[:-1] + ffn_rest[:half] + out_steps[-1:] + ffn_rest[half:] + ffn_steps[:n_up],
                ffn_steps[n_up:])

    first = project(0)
    lru_ready = 2
    _run_interleaved(first[:lru_ready])
    pending, ffn_rest = [], []
    for s in range(n_sub):
        mixers = (_phase_lru(tiles[s], p, xl_scr, hcar_scr)
                  + _phase_gla(tiles[s], p, GLA_CHUNK, scat_scr, tri, unsafe_ref=unsafe_ref))
        others = (first[lru_ready:] if s == 0 else []) + (project(s + 1) if s + 1 < n_sub else [])
        _run_interleaved(mixers, *([others] if others else []), *([pending] if pending else []))
        pending, ffn_rest = after_mixers(s, ffn_rest)
    _run_interleaved(pending + ffn_rest)

    @pl.when(unsafe_ref[0] != 0)
    def _():
        xl_scr[0:SUBLANES, :] = xl_keep[...]
        hcar_scr[...] = hcar_keep[...]
        scat_scr[...] = scat_keep[...]

        def redo(c, carry):
            chunk_rows = pl.ds(pl.multiple_of(c * GLA_CHUNK, GLA_CHUNK), GLA_CHUNK)
            t = types.SimpleNamespace(rows=GLA_CHUNK)

            def store(v):
                y_ref[0, chunk_rows, :] = v

            _run_interleaved(_phase_project(t, lambda: x_ref[0, chunk_rows, :], p)
                             + _phase_lru(t, p, xl_scr, hcar_scr)
                             + _phase_gla(t, p, GLA_CHUNK, scat_scr, safe=True)
                             + _phase_out(t, p) + _phase_ffn(t, fp, store))
            return carry

        lax.fori_loop(0, PROMPT_TILE // GLA_CHUNK, redo, 0)

    @pl.when(step == pl.num_programs(1) - 1)
    def _():
        gla_ref[0] = scat_scr[...]
        lru_ref[0] = hcar_scr[...]
        conv_ref[0] = xl_scr[SUBLANES - (CONV_K - 1):SUBLANES, :]


def _sample_kernel(x_ref, conv_in_ref, lru_in_ref, s_in_ref, *refs):
    p = MixerParams(*refs[:N_MIXER_PARAMS])
    (h1_ref, s_out_ref, lru_out_ref, conv_out_ref,
     hln_scr, ylru_scr, q_scr, k_scr, eg_scr, v_scr, gg_scr, o_scr) = refs[N_MIXER_PARAMS:]
    i = pl.program_id(0)

    @pl.when(i == 0)
    def _():
        h = _layer_norm(x_ref[...], p.ln_g[...], p.ln_b[...])
        hln_scr[...] = h
        proj = _dot(h.astype(bf16), p.w_in[...])
        x_lru = proj[:, COL_XLRU:COL_GLRU]
        cw = p.conv_w[...]
        u = p.conv_b[...] + cw[3:4] * x_lru
        for j in range(CONV_K - 1):
            u = u + cw[j:j + 1] * conv_in_ref[:, j, :]
        for j in range(CONV_K - 2):
            conv_out_ref[:, j, :] = conv_in_ref[:, j + 1, :]
        conv_out_ref[:, CONV_K - 2, :] = x_lru
        h_in = lru_in_ref[...]
        halves = []
        for half in range(LRU_W // MXU_DIM):
            lo, hi = half * MXU_DIM, (half + 1) * MXU_DIM
            a, xin = _lru_gates(u[:, lo:hi], half, p)
            halves.append(a * h_in[:, lo:hi] + xin)
        h_new = jnp.concatenate(halves, axis=1)
        lru_out_ref[...] = h_new
        ylru_scr[...] = h_new * _gelu_tanh(proj[:, COL_GLRU:COL_Q])
        z = _dot(proj[:, COL_ALR:COL_ALR + ALR_PAD].astype(bf16), p.al_w[...]) + p.al_b[...]
        eg_scr[...] = jnp.exp(_gla_log_decay(z))
        q_scr[...] = proj[:, COL_Q:COL_K] * (GLA_DKH ** -0.5)
        k_scr[...] = proj[:, COL_K:COL_V]
        v_scr[...] = proj[:, COL_V:COL_GGLA]
        gg_scr[...] = proj[:, COL_GGLA:COL_ALR]

    r0 = pl.multiple_of(i * SAMPLE_BLOCK, SAMPLE_BLOCK)
    q_t = q_scr[pl.ds(r0, SAMPLE_BLOCK), :].T
    k_t = k_scr[pl.ds(r0, SAMPLE_BLOCK), :].T
    e_t = eg_scr[pl.ds(r0, SAMPLE_BLOCK), :].T
    v_blk = v_scr[pl.ds(r0, SAMPLE_BLOCK), :]
    o_rows = []
    for j in range(SAMPLE_BLOCK):
        heads = []
        for h in range(GLA_H):
            ks = slice(h * GLA_DKH, (h + 1) * GLA_DKH)
            s_new = (e_t[ks, j:j + 1] * s_in_ref[j, h]
                     + k_t[ks, j:j + 1] * v_blk[j:j + 1, h * GLA_DVH:(h + 1) * GLA_DVH])
            s_out_ref[j, h] = s_new
            heads.append(jnp.sum(q_t[ks, j:j + 1] * s_new, axis=0, keepdims=True))
        o_rows.append(jnp.concatenate(heads, axis=1))
    o_scr[pl.ds(r0, SAMPLE_BLOCK), :] = jnp.concatenate(o_rows, axis=0)

    @pl.when(i == pl.num_programs(0) - 1)
    def _():
        y_gla = _gla_finish(o_scr[...], gg_scr[...], p.gn_g[...])
        y = jnp.concatenate([ylru_scr[...], y_gla], axis=1).astype(bf16)
        mix = _dot(y, p.w_out[...])
        h1_ref[...] = _layer_norm(DEEPNORM_ALPHA * hln_scr[...] + mix, p.ln1_g[...], p.ln1_b[...])


def _ffn_kernel(h_ref, *refs):
    fp = FfnParams(*refs[:N_FFN_PARAMS])
    out_ref, = refs[N_FFN_PARAMS:]
    h1 = h_ref[...]
    t = types.SimpleNamespace(h1=h1, h1b=h1.astype(bf16))

    def store(v):
        out_ref[...] = v

    _run_interleaved(_phase_ffn(t, fp, store))


def _block_diag(blocks):
    n, r, c = blocks.shape
    eye = jnp.eye(n, dtype=blocks.dtype)
    return (eye[:, None, :, None] * blocks[:, :, None, :]).reshape(n * r, n * c)


def _const_spec(arr):
    zeros = (0,) * arr.ndim
    return pl.BlockSpec(arr.shape, lambda *_: zeros, pipeline_mode=pl.Buffered(1))


def _mixer_params(ln_in_g, ln_in_b, w_in, conv_w, conv_b, ga_w, ga_b, gx_w, gx_b, lam, al_w, al_b, gn_g, w_out,
                  ln1_g, ln1_b):
    row = lambda a: a.reshape(1, -1).astype(f32)
    w_in_p = jnp.pad(w_in, ((0, 0), (0, IN_COLS_PAD - IN_COLS))).astype(bf16)
    per_half = MXU_DIM // LRU_BLK
    w_gate = jnp.stack([
        jnp.concatenate([_block_diag(ga_w[c * per_half:(c + 1) * per_half]),
                         _block_diag(gx_w[c * per_half:(c + 1) * per_half])], axis=1)
        for c in range(LRU_W // MXU_DIM)]).astype(bf16)
    al_w_p = jnp.pad(al_w, ((0, ALR_PAD - GLA_RANK), (0, 0))).astype(bf16)
    return MixerParams(row(ln_in_g), row(ln_in_b), w_in_p, conv_w.astype(f32), row(conv_b), w_gate, row(ga_b),
                       row(gx_b), row(lam), al_w_p, row(al_b), row(gn_g), w_out.astype(bf16), row(ln1_g), row(ln1_b))


def _meta_call(meta_tokens, params):
    out_shape = (jax.ShapeDtypeStruct((SUBLANES, LRU_W), f32),
                 jax.ShapeDtypeStruct((1, LRU_W), f32),
                 jax.ShapeDtypeStruct((GLA_DK, GLA_DVH), f32))
    return pl.pallas_call(
        _meta_kernel, out_shape=out_shape, name="meta_state",
        scratch_shapes=[pltpu.VMEM((N_META + SUBLANES, LRU_W), f32)],
        compiler_params=pltpu.CompilerParams(vmem_limit_bytes=VMEM_LIMIT_BYTES),
    )(meta_tokens, *params)


def _prompt_call(x, xlc0, h0, s0, params, ffn_params):
    batch, seq, _ = x.shape
    assert seq % PROMPT_TILE == 0 and PROMPT_TILE % PROMPT_SUBTILE == 0 and PROMPT_SUBTILE % GLA_CHUNK == 0
    n_tiles = seq // PROMPT_TILE
    pos = jnp.arange(PROMPT_SUBTILE)
    tri = ((pos[:, None] >= pos[None, :])
           & (pos[:, None] // GLA_CHUNK == pos[None, :] // GLA_CHUNK)).astype(bf16)
    seq_map = lambda b, t: (b, t, 0)
    state_map = lambda b, t: (b, 0, 0)
    out_shape = (jax.ShapeDtypeStruct((batch, seq, D_MODEL), f32),
                 jax.ShapeDtypeStruct((batch, GLA_DK, GLA_DVH), f32),
                 jax.ShapeDtypeStruct((batch, 1, LRU_W), f32),
                 jax.ShapeDtypeStruct((batch, CONV_K - 1, LRU_W), f32))
    return pl.pallas_call(
        _prompt_kernel, out_shape=out_shape, name="prompt_mixer",
        grid=(batch, n_tiles),
        in_specs=[pl.BlockSpec((1, PROMPT_TILE, D_MODEL), seq_map),
                  _const_spec(xlc0), _const_spec(h0), _const_spec(s0), _const_spec(tri)]
                 + [_const_spec(a) for a in params] + [_const_spec(a) for a in ffn_params],
        out_specs=(pl.BlockSpec((1, PROMPT_TILE, D_MODEL), seq_map),
                   pl.BlockSpec((1, GLA_DK, GLA_DVH), state_map),
                   pl.BlockSpec((1, 1, LRU_W), state_map),
                   pl.BlockSpec((1, CONV_K - 1, LRU_W), state_map)),
        scratch_shapes=[pltpu.VMEM((PROMPT_SUBTILE + SUBLANES, LRU_W), f32),
                        pltpu.VMEM((1, LRU_W), f32),
                        pltpu.VMEM((GLA_DK, GLA_DVH), f32),
                        pltpu.VMEM((SUBLANES, LRU_W), f32),
                        pltpu.VMEM((1, LRU_W), f32),
                        pltpu.VMEM((GLA_DK, GLA_DVH), f32),
                        pltpu.SMEM((1,), jnp.int32)],
        compiler_params=pltpu.CompilerParams(dimension_semantics=("arbitrary", "arbitrary"),
                                             vmem_limit_bytes=VMEM_LIMIT_BYTES),
    )(x, xlc0, h0, s0, tri, *params, *ffn_params)


def _sample_call(x, conv_in, lru_in, gla_in, params):
    n = x.shape[0]
    assert n % SAMPLE_BLOCK == 0
    state_block = (SAMPLE_BLOCK, GLA_H, GLA_DKH, GLA_DVH)
    state_map = lambda i: (i, 0, 0, 0)
    out_shape = (jax.ShapeDtypeStruct((n, D_MODEL), f32),
                 jax.ShapeDtypeStruct(gla_in.shape, f32),
                 jax.ShapeDtypeStruct((n, LRU_W), f32),
                 jax.ShapeDtypeStruct((n, CONV_K - 1, LRU_W), f32))
    full = lambda shape: pl.BlockSpec(shape, lambda i: (0,) * len(shape))
    return pl.pallas_call(
        _sample_kernel, out_shape=out_shape, name="sample_mixer",
        grid=(n // SAMPLE_BLOCK,),
        in_specs=[_const_spec(x), _const_spec(conv_in), _const_spec(lru_in),
                  pl.BlockSpec(state_block, state_map)] + [_const_spec(a) for a in params],
        out_specs=(full((n, D_MODEL)), pl.BlockSpec(state_block, state_map), full((n, LRU_W)),
                   full((n, CONV_K - 1, LRU_W))),
        scratch_shapes=[pltpu.VMEM((n, D_MODEL), f32), pltpu.VMEM((n, LRU_W), f32),
                        pltpu.VMEM((n, GLA_DK), f32), pltpu.VMEM((n, GLA_DK), f32), pltpu.VMEM((n, GLA_DK), f32),
                        pltpu.VMEM((n, GLA_DV), f32), pltpu.VMEM((n, GLA_DV), f32), pltpu.VMEM((n, GLA_DV), f32)],
        compiler_params=pltpu.CompilerParams(dimension_semantics=("arbitrary",),
                                             vmem_limit_bytes=VMEM_LIMIT_BYTES),
    )(x, conv_in, lru_in, gla_in, *params)


def _ffn_call(h, ffn_params):
    return pl.pallas_call(
        _ffn_kernel, out_shape=jax.ShapeDtypeStruct(h.shape, f32), name="sample_ffn",
        compiler_params=pltpu.CompilerParams(vmem_limit_bytes=VMEM_LIMIT_BYTES),
    )(h, *ffn_params)


def kernel(x_prompt, x_sample, state_gla, state_lru, state_conv, meta_tokens, ln_in_g, ln_in_b, w_in, conv_w, conv_b, lru_gate_a_w, lru_gate_a_b, lru_gate_x_w, lru_gate_x_b, lru_lambda, gla_alpha_w, gla_alpha_b, gla_norm_g, w_out, ln1_g, ln1_b, w_ffn_gate, w_ffn_up, w_ffn_down, ln2_g, ln2_b):
    assert w_in.shape[0] == 1 and x_sample.shape[1] == 1, "one layer, one decode token per sample sequence"
    batch, seq, _ = x_prompt.shape
    n_sample = x_sample.shape[0]
    params = _mixer_params(ln_in_g, ln_in_b, w_in[0], conv_w[0], conv_b[0], lru_gate_a_w[0], lru_gate_a_b[0],
                           lru_gate_x_w[0], lru_gate_x_b[0], lru_lambda[0], gla_alpha_w[0], gla_alpha_b[0],
                           gla_norm_g[0], w_out[0], ln1_g[0], ln1_b[0])
    ffn_params = FfnParams(w_ffn_gate[0].astype(bf16), w_ffn_up[0].astype(bf16), w_ffn_down[0].astype(bf16),
                           ln2_g[0].reshape(1, -1), ln2_b[0].reshape(1, -1))

    xlc0, h0, s0 = _meta_call(meta_tokens, params)
    y_p, gla_p, lru_p, conv_p = _prompt_call(x_prompt, xlc0, h0, s0, params, ffn_params)
    h1_s, gla_s, lru_s, conv_s = _sample_call(x_sample[:, 0, :], state_conv[0], state_lru[0], state_gla[0], params)
    y_s = _ffn_call(h1_s, ffn_params)
    return (y_p,
            y_s.reshape(n_sample, 1, D_MODEL),
            gla_p.reshape(1, batch, GLA_H, GLA_DKH, GLA_DVH),
            lru_p.reshape(1, batch, LRU_W),
            conv_p[None],
            gla_s[None],
            lru_s[None],
            conv_s[None])
```

```python
import functools
import math
import types
from typing import NamedTuple

import jax
import jax.numpy as jnp
from jax import lax
from jax.experimental import pallas as pl
from jax.experimental.pallas import tpu as pltpu

f32 = jnp.float32
bf16 = jnp.bfloat16

D_MODEL = 1024
N_META = 16
LRU_W = 512
LRU_BLOCKS = 8
LRU_BLK = LRU_W // LRU_BLOCKS
LRU_C = 8.0
CONV_K = 4
GLA_H = 4
GLA_DKH = 64
GLA_DVH = 128
GLA_DK = GLA_H * GLA_DKH
GLA_DV = GLA_H * GLA_DVH
GLA_RANK = 16
GLA_TAU = 16.0
D_FF = 2816
DEEPNORM_ALPHA = 2.0 ** 0.25
LN_EPS = 1e-5
RMS_EPS = 1e-6
GELU_C = math.sqrt(2.0 / math.pi)
GELU_CUBIC = 0.044715
LOG2_E = math.log2(math.e)
F32_TINY = float(jnp.finfo(jnp.float32).tiny)

COL_XLRU = 0
COL_GLRU = COL_XLRU + LRU_W
COL_Q = COL_GLRU + LRU_W
COL_K = COL_Q + GLA_DK
COL_V = COL_K + GLA_DK
COL_GGLA = COL_V + GLA_DV
COL_ALR = COL_GGLA + GLA_DV
IN_COLS = COL_ALR + GLA_RANK

LANES = 128
SUBLANES = 8
MXU_DIM = 256
WIDE_STEP_COLS = 2 * MXU_DIM
ALR_PAD = LANES
IN_COLS_PAD = COL_ALR + ALR_PAD
VMEM_LIMIT_BYTES = 56 * 1024 * 1024

PROMPT_TILE = 512
PROMPT_SUBTILE = 256
GLA_CHUNK = 64
GLA_SAFE_LOG_DECAY = 80.0


class MixerParams(NamedTuple):
    ln_g: jax.Array
    ln_b: jax.Array
    w_in: jax.Array
    conv_w: jax.Array
    conv_b: jax.Array
    w_gate: jax.Array
    gate_a_b: jax.Array
    gate_x_b: jax.Array
    lam: jax.Array
    al_w: jax.Array
    al_b: jax.Array
    gn_g: jax.Array
    w_out: jax.Array
    ln1_g: jax.Array
    ln1_b: jax.Array


N_MIXER_PARAMS = len(MixerParams._fields)


class FfnParams(NamedTuple):
    w_gate: jax.Array
    w_up: jax.Array
    w_down: jax.Array
    ln2_g: jax.Array
    ln2_b: jax.Array


N_FFN_PARAMS = len(FfnParams._fields)


def _layer_norm(x, g, b):
    mu = jnp.mean(x, -1, keepdims=True)
    xc = x - mu
    var = jnp.mean(xc * xc, -1, keepdims=True)
    return xc * lax.rsqrt(var + LN_EPS) * g + b


def _dot(a, b):
    return jnp.dot(a, b, preferred_element_type=f32)


def _dot_nt(a, b):
    return lax.dot_general(a, b, (((1,), (1,)), ((), ())), preferred_element_type=f32)


def _dot_tn(a, b):
    return lax.dot_general(a, b, (((0,), (0,)), ((), ())), preferred_element_type=f32)


def _sqrt_nonneg(y):
    return y * lax.rsqrt(jnp.maximum(y, F32_TINY))


def _gelu_tanh(x):
    k = -2.0 * GELU_C * LOG2_E
    return x * (1.0 / (1.0 + jnp.exp2(x * (k + (k * GELU_CUBIC) * (x * x)))))


def _gla_log_decay(z):
    scale = 1.0 / GLA_TAU
    return (jnp.minimum(z, 0.0) * scale
            - jnp.log2(1.0 + jnp.exp2(jnp.abs(z) * (-LOG2_E))) * (scale / LOG2_E))


def _lru_gates(u_half, half, p):
    lo = half * MXU_DIM
    hi = lo + MXU_DIM
    gates = _dot(u_half.astype(bf16), p.w_gate[half])
    r = jax.nn.sigmoid(gates[:, :MXU_DIM] + p.gate_a_b[:, lo:hi])
    i = jax.nn.sigmoid(gates[:, MXU_DIM:] + p.gate_x_b[:, lo:hi])
    neg_log_a = r * (LRU_C * jax.nn.softplus(-p.lam[:, lo:hi]))
    a = jnp.exp2(neg_log_a * (-LOG2_E))
    gain_sq = jnp.tanh(neg_log_a) * (a * a + 1.0)
    return a, _sqrt_nonneg(gain_sq) * (i * u_half)


def _scan_groups(a, x):
    rows, width = a.shape
    groups = rows // SUBLANES
    a3 = a.reshape(groups, SUBLANES, width)
    x3 = x.reshape(groups, SUBLANES, width)
    sub = lax.broadcasted_iota(jnp.int32, (groups, SUBLANES, width), 1)
    shift = 1
    while shift < SUBLANES:
        keep = sub >= shift
        a_prev = jnp.where(keep, pltpu.roll(a3, shift, 1), 1.0)
        x_prev = jnp.where(keep, pltpu.roll(x3, shift, 1), 0.0)
        x3 = x3 + a3 * x_prev
        a3 = a3 * a_prev
        shift *= 2
    return a3.reshape(rows, width), x3.reshape(rows, width)


def _lru_scan(a, x, h_in):
    rows = a.shape[0]
    big_a, big_x = _scan_groups(a, x)
    carry = h_in
    out = []
    for g in range(rows // SUBLANES):
        sl = slice(g * SUBLANES, (g + 1) * SUBLANES)
        hg = big_a[sl] * carry + big_x[sl]
        carry = hg[SUBLANES - 1:SUBLANES]
        out.append(hg)
    return jnp.concatenate(out, axis=0), carry


def _chunk_cumsum(g, chunk):
    row = lax.broadcasted_iota(jnp.int32, g.shape, 0) % chunk
    shift = 1
    while shift < chunk:
        g = g + jnp.where(row >= shift, pltpu.roll(g, shift, 0), 0.0)
        shift *= 2
    return g


def _chunk_cumsum_mxu(g, tri):
    hi = g.astype(bf16)
    lo = (g - hi.astype(f32)).astype(bf16)
    return _dot(tri, hi) + _dot(tri, lo)


def _head_stack(x, head_masks):
    zero = jnp.zeros_like(x)
    return jnp.concatenate([jnp.where(m, x, zero) for m in head_masks], axis=0)


def _gla_finish(o, g_gla, gn_g):
    heads = []
    for h in range(GLA_H):
        oh = o[:, h * GLA_DVH:(h + 1) * GLA_DVH]
        heads.append(oh * lax.rsqrt(jnp.mean(oh * oh, -1, keepdims=True) + RMS_EPS))
    return jnp.concatenate(heads, axis=1) * gn_g * jax.nn.silu(g_gla)


STEP_COST = dict(
    in_norm=0.6, in_lru=1.0, in_qk=0.5, in_v=0.5, in_g=0.6,
    conv=0.9, gates=0.7, scan=0.8, lru_finish=0.5,
    gla_prepare=0.9, gla_chunk=0.4, gla_finish=0.5,
    out_cols=0.5, out_norm=0.5,
    ffn_up=0.5, ffn_down=1.4, ffn_norm=0.5,
    sample_load=0.05, sample_sequence=0.1,
)


def _phase_project(t, x_of, p):
    def norm():
        t.h = _layer_norm(x_of(), p.ln_g[...], p.ln_b[...])
        t.hb = t.h.astype(bf16)

    def lru():
        t.p_lru = _dot(t.hb, p.w_in[:, COL_XLRU:COL_Q])

    def qk():
        t.p_qk = _dot(t.hb, p.w_in[:, COL_Q:COL_V])

    def v():
        t.p_v = _dot(t.hb, p.w_in[:, COL_V:COL_GGLA])

    def g():
        t.p_g = _dot(t.hb, p.w_in[:, COL_GGLA:IN_COLS_PAD])

    c = STEP_COST
    return [(norm, c["in_norm"]), (lru, c["in_lru"]), (qk, c["in_qk"]), (v, c["in_v"]), (g, c["in_g"])]


def _phase_lru(t, p, xl_ref, hcar_ref, want_out=True):
    halves = range(LRU_W // MXU_DIM)
    t.hs, t.carry, t.gates = {}, {}, {}

    def conv():
        x_lru = t.p_lru[:, :LRU_W]
        rows = x_lru.shape[0]
        xl_ref[SUBLANES:SUBLANES + rows, :] = x_lru
        cw = p.conv_w[...]
        u = p.conv_b[...] + cw[CONV_K - 1:CONV_K] * x_lru
        for j in range(CONV_K - 1):
            start = SUBLANES - (CONV_K - 1) + j
            u = u + cw[j:j + 1] * xl_ref[start:start + rows, :]
        xl_ref[0:SUBLANES, :] = xl_ref[rows:rows + SUBLANES, :]
        t.u = u
        t.h_in = hcar_ref[...]

    def gates(half):
        def step():
            t.gates[half] = _lru_gates(t.u[:, half * MXU_DIM:(half + 1) * MXU_DIM], half, p)
        return step

    def scan(half):
        def step():
            a, xin = t.gates[half]
            t.hs[half], t.carry[half] = _lru_scan(a, xin, t.h_in[:, half * MXU_DIM:(half + 1) * MXU_DIM])
        return step

    def finish():
        hcar_ref[...] = jnp.concatenate([t.carry[h] for h in halves], axis=1)
        if want_out:
            hs = jnp.concatenate([t.hs[h] for h in halves], axis=1)
            t.y_lru = (hs * _gelu_tanh(t.p_lru[:, LRU_W:])).astype(bf16)

    steps = [(conv, STEP_COST["conv"])]
    for half in halves:
        steps += [(gates(half), STEP_COST["gates"]), (scan(half), STEP_COST["scan"])]
    return steps + [(finish, STEP_COST["lru_finish"])]


def _safe_scores(qs, k, b, head_masks):
    chunk = qs.shape[0]
    t_idx = lax.broadcasted_iota(jnp.int32, (GLA_H * chunk, chunk), 0) % chunk
    s_idx = lax.broadcasted_iota(jnp.int32, (GLA_H * chunk, chunk), 1)
    row = lax.broadcasted_iota(jnp.int32, (chunk, chunk), 0)
    col = lax.broadcasted_iota(jnp.int32, (chunk, chunk), 1)
    row_wide = lax.broadcasted_iota(jnp.int32, (chunk, GLA_DK), 0)
    diag = _dot_nt(_head_stack(qs.astype(bf16), head_masks), k.astype(bf16))
    total = jnp.where(t_idx == s_idx, diag, 0.0)
    size = 2
    while size <= chunk:
        half = size // 2
        split_row = (row // size) * size + (half - 1)
        pick = (col == split_row).astype(f32)
        b_split = jnp.dot(pick, b, preferred_element_type=f32, precision=lax.Precision.HIGHEST)
        decay = jnp.exp(-jnp.abs(b - b_split))
        second = (row_wide % size) >= half
        q_part = jnp.where(second, qs * decay, 0.0).astype(bf16)
        k_part = jnp.where(second, 0.0, k * decay).astype(bf16)
        part = _dot_nt(_head_stack(q_part, head_masks), k_part)
        total = total + jnp.where((t_idx // size) == (s_idx // size), part, 0.0)
        size *= 2
    return total


def _phase_gla(t, p, chunk, scat_ref, tri=None, want_out=True, safe=False, unsafe_ref=None):
    outs = []

    def prepare():
        q, k = t.p_qk[:, :GLA_DK], t.p_qk[:, GLA_DK:]
        n_chunks = q.shape[0] // chunk
        z = _dot(t.p_g[:, GLA_DV:].astype(bf16), p.al_w[...]) + p.al_b[...]
        g = _gla_log_decay(z)
        b = _chunk_cumsum(g, chunk) if tri is None else _chunk_cumsum_mxu(g, tri)
        last_rows = [b[(c + 1) * chunk - 1:(c + 1) * chunk] for c in range(n_chunks)]
        b_last = jnp.concatenate([jnp.broadcast_to(r, (chunk, GLA_DK)) for r in last_rows], axis=0)
        qs = q * (GLA_DKH ** -0.5)
        t.q_state = (qs * jnp.exp(b)).astype(bf16)
        t.k_end = (k * jnp.exp(b_last - b)).astype(bf16)
        if safe:
            t.qs, t.k, t.b = qs, k, b
        else:
            t.q_end = (qs * jnp.exp(b - b_last)).astype(bf16)
            if unsafe_ref is not None:
                strongest = jnp.min(jnp.concatenate(last_rows, axis=0))
                unsafe_ref[0] = jnp.where(strongest < -GLA_SAFE_LOG_DECAY, 1, unsafe_ref[0])
        t.vb = t.p_v.astype(bf16)
        pad = jnp.zeros((SUBLANES - n_chunks, GLA_DK), f32)
        t.chunk_decay = jnp.exp(jnp.concatenate(last_rows + [pad], axis=0)).T
        lane = lax.broadcasted_iota(jnp.int32, (chunk, GLA_DK), 1)
        t.head_masks = [(lane // GLA_DKH) == h for h in range(GLA_H)]
        t_idx = lax.broadcasted_iota(jnp.int32, (GLA_H * chunk, chunk), 0) % chunk
        s_idx = lax.broadcasted_iota(jnp.int32, (GLA_H * chunk, chunk), 1)
        t.causal = s_idx <= t_idx

    def one_chunk(c):
        def step():
            sl = slice(c * chunk, (c + 1) * chunk)
            if safe:
                probs = _safe_scores(t.qs[sl], t.k[sl], t.b[sl], t.head_masks).astype(bf16)
            else:
                scores = _dot_nt(_head_stack(t.q_end[sl], t.head_masks), t.k_end[sl])
                probs = jnp.where(t.causal, scores, 0.0).astype(bf16)
            state = scat_ref[...]
            o_state = _dot(_head_stack(t.q_state[sl], t.head_masks), state.astype(bf16))
            v_heads = [t.vb[sl, h * GLA_DVH:(h + 1) * GLA_DVH] for h in range(GLA_H)]
            outs.append(jnp.concatenate(
                [o_state[h * chunk:(h + 1) * chunk] + _dot(probs[h * chunk:(h + 1) * chunk], v_heads[h])
                 for h in range(GLA_H)], axis=1))
            d_state = _dot_tn(_head_stack(t.k_end[sl], t.head_masks), jnp.concatenate(v_heads, axis=0))
            scat_ref[...] = t.chunk_decay[:, c:c + 1] * state + d_state
        return step

    def finish():
        if want_out:
            t.y_gla = _gla_finish(jnp.concatenate(outs, axis=0), t.p_g[:, :GLA_DV], p.gn_g[...]).astype(bf16)

    return ([(prepare, STEP_COST["gla_prepare"])]
            + [(one_chunk(c), STEP_COST["gla_chunk"]) for c in range(t.rows // chunk)]
            + [(finish, STEP_COST["gla_finish"])])


def _phase_out(t, p):
    mix = []

    def project(j):
        def step():
            cols = slice(j * WIDE_STEP_COLS, (j + 1) * WIDE_STEP_COLS)
            mix.append(_dot(t.y_lru, p.w_out[0:LRU_W, cols]) + _dot(t.y_gla, p.w_out[LRU_W:, cols]))
        return step

    def norm():
        t.h1 = _layer_norm(DEEPNORM_ALPHA * t.h + jnp.concatenate(mix, axis=1), p.ln1_g[...], p.ln1_b[...])
        t.h1b = t.h1.astype(bf16)

    return ([(project(j), STEP_COST["out_cols"]) for j in range(D_MODEL // WIDE_STEP_COLS)]
            + [(norm, STEP_COST["out_norm"])])


def _phase_ffn(t, fp, store):
    acts, outs = [], []

    def up(n):
        def step():
            cols = slice(n * MXU_DIM, (n + 1) * MXU_DIM)
            gate = _dot(t.h1b, fp.w_gate[:, cols])
            acts.append((jax.nn.silu(gate) * _dot(t.h1b, fp.w_up[:, cols])).astype(bf16))
        return step

    def down(j):
        def step():
            if j == 0:
                t.act = jnp.concatenate(acts, axis=1)
            outs.append(_dot(t.act, fp.w_down[:, j * WIDE_STEP_COLS:(j + 1) * WIDE_STEP_COLS]))
        return step

    def norm():
        ffn = jnp.concatenate(outs, axis=1)
        store(_layer_norm(DEEPNORM_ALPHA * t.h1 + ffn, fp.ln2_g[...], fp.ln2_b[...]))

    return ([(up(n), STEP_COST["ffn_up"]) for n in range(D_FF // MXU_DIM)]
            + [(down(j), STEP_COST["ffn_down"]) for j in range(D_MODEL // WIDE_STEP_COLS)]
            + [(norm, STEP_COST["ffn_norm"])])


def _run_interleaved(*phases):
    keyed = []
    for n, steps in enumerate(phases):
        total = sum(cost for _, cost in steps)
        done = 0.0
        for step, cost in steps:
            keyed.append(((done + 0.5 * cost) / total, n, step))
            done += cost
    for _, _, step in sorted(keyed, key=lambda e: e[:2]):
        step()


def _pre_kernel(meta_ref, x_ref, conv_in_ref, lru_in_ref, *refs):
    p = MixerParams(*refs[:N_MIXER_PARAMS])
    (xlc_ref, h_ref, s_ref, lru_out_ref, conv_out_ref, hln_ref, ylru_ref, q_ref, k_ref, eg_ref, v_ref, gg_ref,
     xl_scr) = refs[N_MIXER_PARAMS:]
    xl_scr[...] = jnp.zeros_like(xl_scr)
    h_ref[...] = jnp.zeros_like(h_ref)
    s_ref[...] = jnp.zeros_like(s_ref)
    t = types.SimpleNamespace(rows=N_META)
    _run_interleaved(_phase_project(t, lambda: meta_ref[...], p))
    _run_interleaved(_phase_lru(t, p, xl_scr, h_ref, want_out=False))
    _run_interleaved(_phase_gla(t, p, N_META, s_ref, want_out=False, safe=True))
    xlc_ref[...] = xl_scr[0:SUBLANES, :]

    h = _layer_norm(x_ref[...], p.ln_g[...], p.ln_b[...])
    hln_ref[...] = h
    proj = _dot(h.astype(bf16), p.w_in[...])
    x_lru = proj[:, COL_XLRU:COL_GLRU]
    cw = p.conv_w[...]
    u = p.conv_b[...] + cw[CONV_K - 1:CONV_K] * x_lru
    for j in range(CONV_K - 1):
        u = u + cw[j:j + 1] * conv_in_ref[:, j, :]
    for j in range(CONV_K - 2):
        conv_out_ref[:, j, :] = conv_in_ref[:, j + 1, :]
    conv_out_ref[:, CONV_K - 2, :] = x_lru
    h_in = lru_in_ref[...]
    halves = []
    for half in range(LRU_W // MXU_DIM):
        lo, hi = half * MXU_DIM, (half + 1) * MXU_DIM
        a, xin = _lru_gates(u[:, lo:hi], half, p)
        halves.append(a * h_in[:, lo:hi] + xin)
    h_new = jnp.concatenate(halves, axis=1)
    lru_out_ref[...] = h_new
    ylru_ref[...] = h_new * _gelu_tanh(proj[:, COL_GLRU:COL_Q])
    z = _dot(proj[:, COL_ALR:COL_ALR + ALR_PAD].astype(bf16), p.al_w[...]) + p.al_b[...]
    eg_ref[...] = jnp.exp(_gla_log_decay(z))
    q_ref[...] = proj[:, COL_Q:COL_K] * (GLA_DKH ** -0.5)
    k_ref[...] = proj[:, COL_K:COL_V]
    v_ref[...] = proj[:, COL_V:COL_GGLA]
    gg_ref[...] = proj[:, COL_GGLA:COL_ALR]


def _phase_sample_gla(step_id, per_step, q_ref, k_ref, e_ref, v_ref, s_in_ref, s_out_ref, o_ref):
    groups = SUBLANES // per_step
    st = types.SimpleNamespace()
    o_rows = []

    def load():
        group = step_id % groups

        def mine(full, axis):
            parts = [lax.slice_in_dim(full, g * per_step, (g + 1) * per_step, axis=axis) for g in range(groups)]
            out = parts[0]
            for g in range(1, groups):
                out = jnp.where(group == g, parts[g], out)
            return out

        st.q, st.k, st.e = (mine(ref[0].T, 1) for ref in (q_ref, k_ref, e_ref))
        st.v = mine(v_ref[0], 0)

    def one_sequence(j):
        def step():
            heads = []
            for h in range(GLA_H):
                ks = slice(h * GLA_DKH, (h + 1) * GLA_DKH)
                s_new = (st.e[ks, j:j + 1] * s_in_ref[j, h]
                         + st.k[ks, j:j + 1] * st.v[j:j + 1, h * GLA_DVH:(h + 1) * GLA_DVH])
                s_out_ref[j, h] = s_new
                heads.append(jnp.sum(st.q[ks, j:j + 1] * s_new, axis=0, keepdims=True))
            o_rows.append(jnp.concatenate(heads, axis=1))
        return step

    def finish():
        o_ref[0] = jnp.concatenate(o_rows, axis=0)

    return ([(load, STEP_COST["sample_load"])]
            + [(one_sequence(j), STEP_COST["sample_sequence"]) for j in range(per_step)]
            + [(finish, STEP_COST["sample_load"])])


def _prompt_kernel(sample_per_step, x_ref, xlc0_ref, h0_ref, s0_ref, tri_ref, sq_ref, sk_ref, se_ref, sv_ref,
                   sgla_in_ref, *refs):
    p = MixerParams(*refs[:N_MIXER_PARAMS])
    fp = FfnParams(*refs[N_MIXER_PARAMS:N_MIXER_PARAMS + N_FFN_PARAMS])
    (y_ref, gla_ref, lru_ref, conv_ref, sgla_out_ref, so_ref, xl_scr, hcar_scr, scat_scr,
     xl_keep, hcar_keep, scat_keep, unsafe_ref) = refs[N_MIXER_PARAMS + N_FFN_PARAMS:]
    step = pl.program_id(1)
    step_id = pl.program_id(0) * pl.num_programs(1) + step
    sample = _phase_sample_gla(step_id, sample_per_step, sq_ref, sk_ref, se_ref, sv_ref, sgla_in_ref, sgla_out_ref,
                               so_ref)

    @pl.when(step == 0)
    def _():
        xl_scr[0:SUBLANES, :] = xlc0_ref[...]
        hcar_scr[...] = h0_ref[...]
        scat_scr[...] = s0_ref[...]

    xl_keep[...] = xl_scr[0:SUBLANES, :]
    hcar_keep[...] = hcar_scr[...]
    scat_keep[...] = scat_scr[...]
    unsafe_ref[0] = 0

    n_sub = PROMPT_TILE // PROMPT_SUBTILE
    tiles = [types.SimpleNamespace(rows=PROMPT_SUBTILE) for _ in range(n_sub)]
    tri = tri_ref[...]

    def rows(s):
        return slice(s * PROMPT_SUBTILE, (s + 1) * PROMPT_SUBTILE)

    def project(s):
        return _phase_project(tiles[s], lambda: x_ref[0, rows(s), :], p)

    def ffn(s):
        def store(v):
            y_ref[0, rows(s), :] = v
        return _phase_ffn(tiles[s], fp, store)

    n_up = D_FF // MXU_DIM

    def after_mixers(s, ffn_rest):
        out_steps, ffn_steps = _phase_out(tiles[s], p), ffn(s)
        half = len(ffn_rest) // 2
        return (out_steps[:-1] + ffn_rest[:half] + out_steps[-1:] + ffn_rest[half:] + ffn_steps[:n_up],
                ffn_steps[n_up:])

    first = project(0)
    lru_ready = 2
    _run_interleaved(first[:lru_ready])
    pending, ffn_rest = [], []
    for s in range(n_sub):
        mixers = (_phase_lru(tiles[s], p, xl_scr, hcar_scr)
                  + _phase_gla(tiles[s], p, GLA_CHUNK, scat_scr, tri, unsafe_ref=unsafe_ref))
        others = (first[lru_ready:] if s == 0 else []) + (project(s + 1) if s + 1 < n_sub else [])
        _run_interleaved(mixers, *([others] if others else []), *([pending] if pending else []))
        pending, ffn_rest = after_mixers(s, ffn_rest)
    _run_interleaved(pending + ffn_rest, sample)

    @pl.when(unsafe_ref[0] != 0)
    def _():
        xl_scr[0:SUBLANES, :] = xl_keep[...]
        hcar_scr[...] = hcar_keep[...]
        scat_scr[...] = scat_keep[...]

        def redo(c, carry):
            chunk_rows = pl.ds(pl.multiple_of(c * GLA_CHUNK, GLA_CHUNK), GLA_CHUNK)
            t = types.SimpleNamespace(rows=GLA_CHUNK)

            def store(v):
                y_ref[0, chunk_rows, :] = v

            _run_interleaved(_phase_project(t, lambda: x_ref[0, chunk_rows, :], p)
                             + _phase_lru(t, p, xl_scr, hcar_scr)
                             + _phase_gla(t, p, GLA_CHUNK, scat_scr, safe=True)
                             + _phase_out(t, p) + _phase_ffn(t, fp, store))
            return carry

        lax.fori_loop(0, PROMPT_TILE // GLA_CHUNK, redo, 0)

    @pl.when(step == pl.num_programs(1) - 1)
    def _():
        gla_ref[0] = scat_scr[...]
        lru_ref[0] = hcar_scr[...]
        conv_ref[0] = xl_scr[SUBLANES - (CONV_K - 1):SUBLANES, :]


def _post_kernel(hln_ref, ylru_ref, gg_ref, o_ref, gn_g_ref, w_out_ref, ln1_g_ref, ln1_b_ref, *refs):
    fp = FfnParams(*refs[:N_FFN_PARAMS])
    out_ref, = refs[N_FFN_PARAMS:]
    y_gla = _gla_finish(o_ref[...], gg_ref[...], gn_g_ref[...])
    y = jnp.concatenate([ylru_ref[...], y_gla], axis=1).astype(bf16)
    h1 = _layer_norm(DEEPNORM_ALPHA * hln_ref[...] + _dot(y, w_out_ref[...]), ln1_g_ref[...], ln1_b_ref[...])
    t = types.SimpleNamespace(h1=h1, h1b=h1.astype(bf16))

    def store(v):
        out_ref[...] = v

    _run_interleaved(_phase_ffn(t, fp, store))


def _block_diag(blocks):
    n, r, c = blocks.shape
    eye = jnp.eye(n, dtype=blocks.dtype)
    return (eye[:, None, :, None] * blocks[:, :, None, :]).reshape(n * r, n * c)


def _const_spec(arr):
    zeros = (0,) * arr.ndim
    return pl.BlockSpec(arr.shape, lambda *_: zeros, pipeline_mode=pl.Buffered(1))


def _mixer_params(ln_in_g, ln_in_b, w_in, conv_w, conv_b, ga_w, ga_b, gx_w, gx_b, lam, al_w, al_b, gn_g, w_out,
                  ln1_g, ln1_b):
    row = lambda a: a.reshape(1, -1).astype(f32)
    w_in_p = jnp.pad(w_in, ((0, 0), (0, IN_COLS_PAD - IN_COLS))).astype(bf16)
    per_half = MXU_DIM // LRU_BLK
    w_gate = jnp.stack([
        jnp.concatenate([_block_diag(ga_w[c * per_half:(c + 1) * per_half]),
                         _block_diag(gx_w[c * per_half:(c + 1) * per_half])], axis=1)
        for c in range(LRU_W // MXU_DIM)]).astype(bf16)
    al_w_p = jnp.pad(al_w, ((0, ALR_PAD - GLA_RANK), (0, 0))).astype(bf16)
    return MixerParams(row(ln_in_g), row(ln_in_b), w_in_p, conv_w.astype(f32), row(conv_b), w_gate, row(ga_b),
                       row(gx_b), row(lam), al_w_p, row(al_b), row(gn_g), w_out.astype(bf16), row(ln1_g), row(ln1_b))


def _pre_call(meta_tokens, x_sample, conv_in, lru_in, params):
    n = x_sample.shape[0]
    sds = lambda *shape: jax.ShapeDtypeStruct(shape, f32)
    out_shape = (sds(SUBLANES, LRU_W), sds(1, LRU_W), sds(GLA_DK, GLA_DVH),
                 sds(n, LRU_W), sds(n, CONV_K - 1, LRU_W),
                 sds(n, D_MODEL), sds(n, LRU_W),
                 sds(n, GLA_DK), sds(n, GLA_DK), sds(n, GLA_DK), sds(n, GLA_DV), sds(n, GLA_DV))
    return pl.pallas_call(
        _pre_kernel, out_shape=out_shape, name="meta_and_sample_pre",
        scratch_shapes=[pltpu.VMEM((N_META + SUBLANES, LRU_W), f32)],
        compiler_params=pltpu.CompilerParams(vmem_limit_bytes=VMEM_LIMIT_BYTES),
    )(meta_tokens, x_sample, conv_in, lru_in, *params)


def _prompt_call(x, xlc0, h0, s0, sample_q, sample_k, sample_decay, sample_v, sample_gla, params, ffn_params):
    batch, seq, _ = x.shape
    assert seq % PROMPT_TILE == 0 and PROMPT_TILE % PROMPT_SUBTILE == 0 and PROMPT_SUBTILE % GLA_CHUNK == 0
    n_tiles = seq // PROMPT_TILE
    n_steps = batch * n_tiles
    n_sample = sample_q.shape[0]
    per_step = n_sample // n_steps
    assert per_step * n_steps == n_sample and SUBLANES % per_step == 0 and n_sample % SUBLANES == 0
    groups = SUBLANES // per_step
    pos = jnp.arange(PROMPT_SUBTILE)
    tri = ((pos[:, None] >= pos[None, :])
           & (pos[:, None] // GLA_CHUNK == pos[None, :] // GLA_CHUNK)).astype(bf16)
    seq_map = lambda b, t: (b, t, 0)
    state_map = lambda b, t: (b, 0, 0)
    rows8_map = lambda b, t: ((b * n_tiles + t) // groups, 0, 0)
    step_map3 = lambda b, t: (b * n_tiles + t, 0, 0)
    step_map4 = lambda b, t: (b * n_tiles + t, 0, 0, 0)
    rows8 = lambda a: a.reshape(n_sample // SUBLANES, SUBLANES, a.shape[-1])
    sample_state_block = (per_step, GLA_H, GLA_DKH, GLA_DVH)
    out_shape = (jax.ShapeDtypeStruct((batch, seq, D_MODEL), f32),
                 jax.ShapeDtypeStruct((batch, GLA_DK, GLA_DVH), f32),
                 jax.ShapeDtypeStruct((batch, 1, LRU_W), f32),
                 jax.ShapeDtypeStruct((batch, CONV_K - 1, LRU_W), f32),
                 jax.ShapeDtypeStruct(sample_gla.shape, f32),
                 jax.ShapeDtypeStruct((n_steps, per_step, GLA_DV), f32))
    y, gla, lru, conv, sample_gla_new, sample_o = pl.pallas_call(
        functools.partial(_prompt_kernel, per_step), out_shape=out_shape, name="prompt_layer",
        grid=(batch, n_tiles),
        in_specs=[pl.BlockSpec((1, PROMPT_TILE, D_MODEL), seq_map),
                  _const_spec(xlc0), _const_spec(h0), _const_spec(s0), _const_spec(tri),
                  pl.BlockSpec((1, SUBLANES, GLA_DK), rows8_map), pl.BlockSpec((1, SUBLANES, GLA_DK), rows8_map),
                  pl.BlockSpec((1, SUBLANES, GLA_DK), rows8_map), pl.BlockSpec((1, SUBLANES, GLA_DV), rows8_map),
                  pl.BlockSpec(sample_state_block, step_map4)]
                 + [_const_spec(a) for a in params] + [_const_spec(a) for a in ffn_params],
        out_specs=(pl.BlockSpec((1, PROMPT_TILE, D_MODEL), seq_map),
                   pl.BlockSpec((1, GLA_DK, GLA_DVH), state_map),
                   pl.BlockSpec((1, 1, LRU_W), state_map),
                   pl.BlockSpec((1, CONV_K - 1, LRU_W), state_map),
                   pl.BlockSpec(sample_state_block, step_map4),
                   pl.BlockSpec((1, per_step, GLA_DV), step_map3)),
        scratch_shapes=[pltpu.VMEM((PROMPT_SUBTILE + SUBLANES, LRU_W), f32),
                        pltpu.VMEM((1, LRU_W), f32),
                        pltpu.VMEM((GLA_DK, GLA_DVH), f32),
                        pltpu.VMEM((SUBLANES, LRU_W), f32),
                        pltpu.VMEM((1, LRU_W), f32),
                        pltpu.VMEM((GLA_DK, GLA_DVH), f32),
                        pltpu.SMEM((1,), jnp.int32)],
        compiler_params=pltpu.CompilerParams(dimension_semantics=("arbitrary", "arbitrary"),
                                             vmem_limit_bytes=VMEM_LIMIT_BYTES),
    )(x, xlc0, h0, s0, tri, rows8(sample_q), rows8(sample_k), rows8(sample_decay), rows8(sample_v), sample_gla,
      *params, *ffn_params)
    return (y, gla, lru, conv), (sample_gla_new, sample_o.reshape(n_sample, GLA_DV))


def _post_call(hln, ylru, gate, o, params, ffn_params):
    return pl.pallas_call(
        _post_kernel, out_shape=jax.ShapeDtypeStruct(hln.shape, f32), name="sample_post",
        compiler_params=pltpu.CompilerParams(vmem_limit_bytes=VMEM_LIMIT_BYTES),
    )(hln, ylru, gate, o, params.gn_g, params.w_out, params.ln1_g, params.ln1_b, *ffn_params)


def kernel(x_prompt, x_sample, state_gla, state_lru, state_conv, meta_tokens, ln_in_g, ln_in_b, w_in, conv_w, conv_b, lru_gate_a_w, lru_gate_a_b, lru_gate_x_w, lru_gate_x_b, lru_lambda, gla_alpha_w, gla_alpha_b, gla_norm_g, w_out, ln1_g, ln1_b, w_ffn_gate, w_ffn_up, w_ffn_down, ln2_g, ln2_b):
    assert w_in.shape[0] == 1 and x_sample.shape[1] == 1, "one layer, one decode token per sample sequence"
    batch, seq, _ = x_prompt.shape
    n_sample = x_sample.shape[0]
    params = _mixer_params(ln_in_g, ln_in_b, w_in[0], conv_w[0], conv_b[0], lru_gate_a_w[0], lru_gate_a_b[0],
                           lru_gate_x_w[0], lru_gate_x_b[0], lru_lambda[0], gla_alpha_w[0], gla_alpha_b[0],
                           gla_norm_g[0], w_out[0], ln1_g[0], ln1_b[0])
    ffn_params = FfnParams(w_ffn_gate[0].astype(bf16), w_ffn_up[0].astype(bf16), w_ffn_down[0].astype(bf16),
                           ln2_g[0].reshape(1, -1), ln2_b[0].reshape(1, -1))

    (xlc0, h0, s0, lru_s, conv_s, hln_s, ylru_s, q_s, k_s, decay_s, v_s, gate_s) = _pre_call(
        meta_tokens, x_sample[:, 0, :], state_conv[0], state_lru[0], params)
    (y_p, gla_p, lru_p, conv_p), (gla_s, o_s) = _prompt_call(
        x_prompt, xlc0, h0, s0, q_s, k_s, decay_s, v_s, state_gla[0], params, ffn_params)
    y_s = _post_call(hln_s, ylru_s, gate_s, o_s, params, ffn_params)
    return (y_p,
            y_s.reshape(n_sample, 1, D_MODEL),
            gla_p.reshape(1, batch, GLA_H, GLA_DKH, GLA_DVH),
            lru_p.reshape(1, batch, LRU_W),
            conv_p[None],
            gla_s[None],
            lru_s[None],
            conv_s[None])
```

```python
import functools
import math
import types
from typing import NamedTuple

import jax
import jax.numpy as jnp
from jax import lax
from jax.experimental import pallas as pl
from jax.experimental.pallas import tpu as pltpu

f32 = jnp.float32
bf16 = jnp.bfloat16

D_MODEL = 1024
N_META = 16
LRU_W = 512
LRU_BLOCKS = 8
LRU_BLK = LRU_W // LRU_BLOCKS
LRU_C = 8.0
CONV_K = 4
GLA_H = 4
GLA_DKH = 64
GLA_DVH = 128
GLA_DK = GLA_H * GLA_DKH
GLA_DV = GLA_H * GLA_DVH
GLA_RANK = 16
GLA_TAU = 16.0
D_FF = 2816
DEEPNORM_ALPHA = 2.0 ** 0.25
LN_EPS = 1e-5
RMS_EPS = 1e-6
GELU_C = math.sqrt(2.0 / math.pi)
GELU_CUBIC = 0.044715
LOG2_E = math.log2(math.e)
F32_TINY = float(jnp.finfo(jnp.float32).tiny)

COL_XLRU = 0
COL_GLRU = COL_XLRU + LRU_W
COL_Q = COL_GLRU + LRU_W
COL_K = COL_Q + GLA_DK
COL_V = COL_K + GLA_DK
COL_GGLA = COL_V + GLA_DV
COL_ALR = COL_GGLA + GLA_DV
IN_COLS = COL_ALR + GLA_RANK

LANES = 128
SUBLANES = 8
MXU_DIM = 256
WIDE_STEP_COLS = 2 * MXU_DIM
ALR_PAD = LANES
IN_COLS_PAD = COL_ALR + ALR_PAD
VMEM_LIMIT_BYTES = 56 * 1024 * 1024

PROMPT_TILE = 512
PROMPT_SUBTILE = 256
GLA_CHUNK = 128
FALLBACK_ROWS = 64
GLA_SAFE_LOG_DECAY = 80.0


class MixerParams(NamedTuple):
    ln_g: jax.Array
    ln_b: jax.Array
    w_in: jax.Array
    conv_w: jax.Array
    conv_b: jax.Array
    w_gate: jax.Array
    gate_a_b: jax.Array
    gate_x_b: jax.Array
    lam: jax.Array
    al_w: jax.Array
    al_b: jax.Array
    gn_g: jax.Array
    w_out: jax.Array
    ln1_g: jax.Array
    ln1_b: jax.Array


N_MIXER_PARAMS = len(MixerParams._fields)


class FfnParams(NamedTuple):
    w_gate: jax.Array
    w_up: jax.Array
    w_down: jax.Array
    ln2_g: jax.Array
    ln2_b: jax.Array


N_FFN_PARAMS = len(FfnParams._fields)


def _layer_norm(x, g, b):
    mu = jnp.mean(x, -1, keepdims=True)
    xc = x - mu
    var = jnp.mean(xc * xc, -1, keepdims=True)
    return xc * lax.rsqrt(var + LN_EPS) * g + b


def _dot(a, b):
    return jnp.dot(a, b, preferred_element_type=f32)


def _dot_nt(a, b):
    return lax.dot_general(a, b, (((1,), (1,)), ((), ())), preferred_element_type=f32)


def _dot_tn(a, b):
    return lax.dot_general(a, b, (((0,), (0,)), ((), ())), preferred_element_type=f32)


def _sqrt_nonneg(y):
    return y * lax.rsqrt(jnp.maximum(y, F32_TINY))


def _gelu_tanh(x):
    k = -2.0 * GELU_C * LOG2_E
    return x * (1.0 / (1.0 + jnp.exp2(x * (k + (k * GELU_CUBIC) * (x * x)))))


def _gla_log_decay(z):
    scale = 1.0 / GLA_TAU
    return (jnp.minimum(z, 0.0) * scale
            - jnp.log2(1.0 + jnp.exp2(jnp.abs(z) * (-LOG2_E))) * (scale / LOG2_E))


def _lru_gates(u_half, half, p):
    lo = half * MXU_DIM
    hi = lo + MXU_DIM
    gates = _dot(u_half.astype(bf16), p.w_gate[half])
    r = jax.nn.sigmoid(gates[:, :MXU_DIM] + p.gate_a_b[:, lo:hi])
    i = jax.nn.sigmoid(gates[:, MXU_DIM:] + p.gate_x_b[:, lo:hi])
    neg_log_a = r * (LRU_C * jax.nn.softplus(-p.lam[:, lo:hi]))
    a = jnp.exp2(neg_log_a * (-LOG2_E))
    gain_sq = jnp.tanh(neg_log_a) * (a * a + 1.0)
    return a, _sqrt_nonneg(gain_sq) * (i * u_half)


def _scan_groups(a, x):
    rows, width = a.shape
    groups = rows // SUBLANES
    a3 = a.reshape(groups, SUBLANES, width)
    x3 = x.reshape(groups, SUBLANES, width)
    sub = lax.broadcasted_iota(jnp.int32, (groups, SUBLANES, width), 1)
    shift = 1
    while shift < SUBLANES:
        keep = sub >= shift
        a_prev = jnp.where(keep, pltpu.roll(a3, shift, 1), 1.0)
        x_prev = jnp.where(keep, pltpu.roll(x3, shift, 1), 0.0)
        x3 = x3 + a3 * x_prev
        a3 = a3 * a_prev
        shift *= 2
    return a3.reshape(rows, width), x3.reshape(rows, width)


def _lru_scan(a, x, h_in):
    rows = a.shape[0]
    big_a, big_x = _scan_groups(a, x)
    carry = h_in
    out = []
    for g in range(rows // SUBLANES):
        sl = slice(g * SUBLANES, (g + 1) * SUBLANES)
        hg = big_a[sl] * carry + big_x[sl]
        carry = hg[SUBLANES - 1:SUBLANES]
        out.append(hg)
    return jnp.concatenate(out, axis=0), carry


def _chunk_cumsum(g, chunk):
    row = lax.broadcasted_iota(jnp.int32, g.shape, 0) % chunk
    shift = 1
    while shift < chunk:
        g = g + jnp.where(row >= shift, pltpu.roll(g, shift, 0), 0.0)
        shift *= 2
    return g


def _chunk_cumsum_mxu(g, tri):
    hi = g.astype(bf16)
    lo = (g - hi.astype(f32)).astype(bf16)
    return _dot(tri, hi) + _dot(tri, lo)


def _head_stack(x, head_masks):
    zero = jnp.zeros_like(x)
    return jnp.concatenate([jnp.where(m, x, zero) for m in head_masks], axis=0)


def _gla_finish(o, g_gla, gn_g):
    heads = []
    for h in range(GLA_H):
        oh = o[:, h * GLA_DVH:(h + 1) * GLA_DVH]
        heads.append(oh * lax.rsqrt(jnp.mean(oh * oh, -1, keepdims=True) + RMS_EPS))
    return jnp.concatenate(heads, axis=1) * gn_g * jax.nn.silu(g_gla)


STEP_COST = dict(
    in_norm=0.6, in_lru=1.0, in_qk=0.5, in_v=0.5, in_g=0.6,
    conv=0.9, gates=0.7, scan=0.8, lru_finish=0.5,
    gla_prepare=0.9, gla_chunk=0.4, gla_finish=0.5,
    out_cols=0.5, out_norm=0.5,
    ffn_up=0.5, ffn_down=1.4, ffn_norm=0.5,
    sample_load=0.05, sample_sequence=0.1,
)


def _phase_project(t, x_of, p):
    def norm():
        t.h = _layer_norm(x_of(), p.ln_g[...], p.ln_b[...])
        t.hb = t.h.astype(bf16)

    def lru():
        t.p_lru = _dot(t.hb, p.w_in[:, COL_XLRU:COL_Q])

    def qk():
        t.p_qk = _dot(t.hb, p.w_in[:, COL_Q:COL_V])

    def v():
        t.p_v = _dot(t.hb, p.w_in[:, COL_V:COL_GGLA])

    def g():
        t.p_g = _dot(t.hb, p.w_in[:, COL_GGLA:IN_COLS_PAD])

    c = STEP_COST
    return [(norm, c["in_norm"]), (lru, c["in_lru"]), (qk, c["in_qk"]), (v, c["in_v"]), (g, c["in_g"])]


def _phase_lru(t, p, xl_ref, hcar_ref, want_out=True):
    halves = range(LRU_W // MXU_DIM)
    t.hs, t.carry, t.gates = {}, {}, {}

    def conv():
        x_lru = t.p_lru[:, :LRU_W]
        rows = x_lru.shape[0]
        xl_ref[SUBLANES:SUBLANES + rows, :] = x_lru
        cw = p.conv_w[...]
        u = p.conv_b[...] + cw[CONV_K - 1:CONV_K] * x_lru
        for j in range(CONV_K - 1):
            start = SUBLANES - (CONV_K - 1) + j
            u = u + cw[j:j + 1] * xl_ref[start:start + rows, :]
        xl_ref[0:SUBLANES, :] = xl_ref[rows:rows + SUBLANES, :]
        t.u = u
        t.h_in = hcar_ref[...]

    def gates(half):
        def step():
            t.gates[half] = _lru_gates(t.u[:, half * MXU_DIM:(half + 1) * MXU_DIM], half, p)
        return step

    def scan(half):
        def step():
            a, xin = t.gates[half]
            t.hs[half], t.carry[half] = _lru_scan(a, xin, t.h_in[:, half * MXU_DIM:(half + 1) * MXU_DIM])
        return step

    def finish():
        hcar_ref[...] = jnp.concatenate([t.carry[h] for h in halves], axis=1)
        if want_out:
            hs = jnp.concatenate([t.hs[h] for h in halves], axis=1)
            t.y_lru = (hs * _gelu_tanh(t.p_lru[:, LRU_W:])).astype(bf16)

    steps = [(conv, STEP_COST["conv"])]
    for half in halves:
        steps += [(gates(half), STEP_COST["gates"]), (scan(half), STEP_COST["scan"])]
    return steps + [(finish, STEP_COST["lru_finish"])]


def _safe_scores(qs, k, b, head_masks):
    chunk = qs.shape[0]
    t_idx = lax.broadcasted_iota(jnp.int32, (GLA_H * chunk, chunk), 0) % chunk
    s_idx = lax.broadcasted_iota(jnp.int32, (GLA_H * chunk, chunk), 1)
    row = lax.broadcasted_iota(jnp.int32, (chunk, chunk), 0)
    col = lax.broadcasted_iota(jnp.int32, (chunk, chunk), 1)
    row_wide = lax.broadcasted_iota(jnp.int32, (chunk, GLA_DK), 0)
    diag = _dot_nt(_head_stack(qs.astype(bf16), head_masks), k.astype(bf16))
    total = jnp.where(t_idx == s_idx, diag, 0.0)
    size = 2
    while size <= chunk:
        half = size // 2
        split_row = (row // size) * size + (half - 1)
        pick = (col == split_row).astype(f32)
        b_split = jnp.dot(pick, b, preferred_element_type=f32, precision=lax.Precision.HIGHEST)
        decay = jnp.exp(-jnp.abs(b - b_split))
        second = (row_wide % size) >= half
        q_part = jnp.where(second, qs * decay, 0.0).astype(bf16)
        k_part = jnp.where(second, 0.0, k * decay).astype(bf16)
        part = _dot_nt(_head_stack(q_part, head_masks), k_part)
        total = total + jnp.where((t_idx // size) == (s_idx // size), part, 0.0)
        size *= 2
    return total


def _phase_gla(t, p, chunk, scat_ref, tri=None, want_out=True, safe=False, unsafe_ref=None):
    outs = []

    def prepare():
        q, k = t.p_qk[:, :GLA_DK], t.p_qk[:, GLA_DK:]
        n_chunks = q.shape[0] // chunk
        z = _dot(t.p_g[:, GLA_DV:].astype(bf16), p.al_w[...]) + p.al_b[...]
        g = _gla_log_decay(z)
        b = _chunk_cumsum(g, chunk) if tri is None else _chunk_cumsum_mxu(g, tri)
        last_rows = [b[(c + 1) * chunk - 1:(c + 1) * chunk] for c in range(n_chunks)]
        b_last = jnp.concatenate([jnp.broadcast_to(r, (chunk, GLA_DK)) for r in last_rows], axis=0)
        qs = q * (GLA_DKH ** -0.5)
        t.q_state = (qs * jnp.exp(b)).astype(bf16)
        t.k_end = (k * jnp.exp(b_last - b)).astype(bf16)
        if safe:
            t.qs, t.k, t.b = qs, k, b
        else:
            t.q_end = (qs * jnp.exp(b - b_last)).astype(bf16)
            if unsafe_ref is not None:
                strongest = jnp.min(jnp.concatenate(last_rows, axis=0))
                unsafe_ref[0] = jnp.where(strongest < -GLA_SAFE_LOG_DECAY, 1, unsafe_ref[0])
        t.vb = t.p_v.astype(bf16)
        pad = jnp.zeros((SUBLANES - n_chunks, GLA_DK), f32)
        t.chunk_decay = jnp.exp(jnp.concatenate(last_rows + [pad], axis=0)).T
        lane = lax.broadcasted_iota(jnp.int32, (chunk, GLA_DK), 1)
        t.head_masks = [(lane // GLA_DKH) == h for h in range(GLA_H)]
        t_idx = lax.broadcasted_iota(jnp.int32, (GLA_H * chunk, chunk), 0) % chunk
        s_idx = lax.broadcasted_iota(jnp.int32, (GLA_H * chunk, chunk), 1)
        t.causal = s_idx <= t_idx

    def one_chunk(c):
        def step():
            sl = slice(c * chunk, (c + 1) * chunk)
            if safe:
                probs = _safe_scores(t.qs[sl], t.k[sl], t.b[sl], t.head_masks).astype(bf16)
            else:
                scores = _dot_nt(_head_stack(t.q_end[sl], t.head_masks), t.k_end[sl])
                probs = jnp.where(t.causal, scores, 0.0).astype(bf16)
            state = scat_ref[...]
            o_state = _dot(_head_stack(t.q_state[sl], t.head_masks), state.astype(bf16))
            v_heads = [t.vb[sl, h * GLA_DVH:(h + 1) * GLA_DVH] for h in range(GLA_H)]
            outs.append(jnp.concatenate(
                [o_state[h * chunk:(h + 1) * chunk] + _dot(probs[h * chunk:(h + 1) * chunk], v_heads[h])
                 for h in range(GLA_H)], axis=1))
            d_state = _dot_tn(_head_stack(t.k_end[sl], t.head_masks), jnp.concatenate(v_heads, axis=0))
            scat_ref[...] = t.chunk_decay[:, c:c + 1] * state + d_state
        return step

    def finish():
        if want_out:
            t.y_gla = _gla_finish(jnp.concatenate(outs, axis=0), t.p_g[:, :GLA_DV], p.gn_g[...]).astype(bf16)

    return ([(prepare, STEP_COST["gla_prepare"])]
            + [(one_chunk(c), STEP_COST["gla_chunk"]) for c in range(t.rows // chunk)]
            + [(finish, STEP_COST["gla_finish"])])


def _phase_out(t, p):
    mix = []

    def project(j):
        def step():
            cols = slice(j * WIDE_STEP_COLS, (j + 1) * WIDE_STEP_COLS)
            mix.append(_dot(t.y_lru, p.w_out[0:LRU_W, cols]) + _dot(t.y_gla, p.w_out[LRU_W:, cols]))
        return step

    def norm():
        t.h1 = _layer_norm(DEEPNORM_ALPHA * t.h + jnp.concatenate(mix, axis=1), p.ln1_g[...], p.ln1_b[...])
        t.h1b = t.h1.astype(bf16)

    return ([(project(j), STEP_COST["out_cols"]) for j in range(D_MODEL // WIDE_STEP_COLS)]
            + [(norm, STEP_COST["out_norm"])])


def _phase_ffn(t, fp, store):
    acts, outs = [], []

    def up(n):
        def step():
            cols = slice(n * MXU_DIM, (n + 1) * MXU_DIM)
            gate = _dot(t.h1b, fp.w_gate[:, cols])
            acts.append((jax.nn.silu(gate) * _dot(t.h1b, fp.w_up[:, cols])).astype(bf16))
        return step

    def down(j):
        def step():
            if j == 0:
                t.act = jnp.concatenate(acts, axis=1)
            outs.append(_dot(t.act, fp.w_down[:, j * WIDE_STEP_COLS:(j + 1) * WIDE_STEP_COLS]))
        return step

    def norm():
        ffn = jnp.concatenate(outs, axis=1)
        store(_layer_norm(DEEPNORM_ALPHA * t.h1 + ffn, fp.ln2_g[...], fp.ln2_b[...]))

    return ([(up(n), STEP_COST["ffn_up"]) for n in range(D_FF // MXU_DIM)]
            + [(down(j), STEP_COST["ffn_down"]) for j in range(D_MODEL // WIDE_STEP_COLS)]
            + [(norm, STEP_COST["ffn_norm"])])


def _run_interleaved(*phases):
    keyed = []
    for n, steps in enumerate(phases):
        total = sum(cost for _, cost in steps)
        done = 0.0
        for step, cost in steps:
            keyed.append(((done + 0.5 * cost) / total, n, step))
            done += cost
    for _, _, step in sorted(keyed, key=lambda e: e[:2]):
        step()


def _pre_kernel(meta_ref, x_ref, conv_in_ref, lru_in_ref, *refs):
    p = MixerParams(*refs[:N_MIXER_PARAMS])
    (xlc_ref, h_ref, s_ref, lru_out_ref, conv_out_ref, hln_ref, ylru_ref, q_ref, k_ref, eg_ref, v_ref, gg_ref,
     xl_scr) = refs[N_MIXER_PARAMS:]
    xl_scr[...] = jnp.zeros_like(xl_scr)
    h_ref[...] = jnp.zeros_like(h_ref)
    s_ref[...] = jnp.zeros_like(s_ref)
    t = types.SimpleNamespace(rows=N_META)
    _run_interleaved(_phase_project(t, lambda: meta_ref[...], p))
    _run_interleaved(_phase_lru(t, p, xl_scr, h_ref, want_out=False))
    _run_interleaved(_phase_gla(t, p, N_META, s_ref, want_out=False, safe=True))
    xlc_ref[...] = xl_scr[0:SUBLANES, :]

    h = _layer_norm(x_ref[:, 0, :], p.ln_g[...], p.ln_b[...])
    hln_ref[...] = h
    proj = _dot(h.astype(bf16), p.w_in[...])
    x_lru = proj[:, COL_XLRU:COL_GLRU]
    cw = p.conv_w[...]
    u = p.conv_b[...] + cw[CONV_K - 1:CONV_K] * x_lru
    for j in range(CONV_K - 1):
        u = u + cw[j:j + 1] * conv_in_ref[j]
    for j in range(CONV_K - 2):
        conv_out_ref[j] = conv_in_ref[j + 1]
    conv_out_ref[CONV_K - 2] = x_lru
    h_in = lru_in_ref[...]
    halves = []
    for half in range(LRU_W // MXU_DIM):
        lo, hi = half * MXU_DIM, (half + 1) * MXU_DIM
        a, xin = _lru_gates(u[:, lo:hi], half, p)
        halves.append(a * h_in[:, lo:hi] + xin)
    h_new = jnp.concatenate(halves, axis=1)
    lru_out_ref[...] = h_new
    ylru_ref[...] = h_new * _gelu_tanh(proj[:, COL_GLRU:COL_Q])
    z = _dot(proj[:, COL_ALR:COL_ALR + ALR_PAD].astype(bf16), p.al_w[...]) + p.al_b[...]
    eg_ref[...] = jnp.exp(_gla_log_decay(z))
    q_ref[...] = proj[:, COL_Q:COL_K] * (GLA_DKH ** -0.5)
    k_ref[...] = proj[:, COL_K:COL_V]
    v_ref[...] = proj[:, COL_V:COL_GGLA]
    gg_ref[...] = proj[:, COL_GGLA:COL_ALR]


def _phase_sample_gla(step_id, per_step, q_ref, k_ref, e_ref, v_ref, s_in_ref, s_out_ref, o_ref):
    groups = SUBLANES // per_step
    st = types.SimpleNamespace()
    o_rows = []

    def load():
        group = step_id % groups

        def mine(full, axis):
            parts = [lax.slice_in_dim(full, g * per_step, (g + 1) * per_step, axis=axis) for g in range(groups)]
            out = parts[0]
            for g in range(1, groups):
                out = jnp.where(group == g, parts[g], out)
            return out

        st.q, st.k, st.e = (mine(ref[0].T, 1) for ref in (q_ref, k_ref, e_ref))
        st.v = mine(v_ref[0], 0)

    def one_sequence(j):
        def step():
            heads = []
            for h in range(GLA_H):
                ks = slice(h * GLA_DKH, (h + 1) * GLA_DKH)
                s_new = (st.e[ks, j:j + 1] * s_in_ref[j, h]
                         + st.k[ks, j:j + 1] * st.v[j:j + 1, h * GLA_DVH:(h + 1) * GLA_DVH])
                s_out_ref[j, h] = s_new
                heads.append(jnp.sum(st.q[ks, j:j + 1] * s_new, axis=0, keepdims=True))
            o_rows.append(jnp.concatenate(heads, axis=1))
        return step

    def finish():
        o_ref[0] = jnp.concatenate(o_rows, axis=0)

    return ([(load, STEP_COST["sample_load"])]
            + [(one_sequence(j), STEP_COST["sample_sequence"]) for j in range(per_step)]
            + [(finish, STEP_COST["sample_load"])])


def _prompt_kernel(sample_per_step, x_ref, xlc0_ref, h0_ref, s0_ref, tri_ref, sq_ref, sk_ref, se_ref, sv_ref,
                   sgla_in_ref, *refs):
    p = MixerParams(*refs[:N_MIXER_PARAMS])
    fp = FfnParams(*refs[N_MIXER_PARAMS:N_MIXER_PARAMS + N_FFN_PARAMS])
    (y_ref, gla_ref, lru_ref, conv_ref, sgla_out_ref, so_ref, xl_scr, hcar_scr, scat_scr,
     xl_keep, hcar_keep, scat_keep, unsafe_ref) = refs[N_MIXER_PARAMS + N_FFN_PARAMS:]
    step = pl.program_id(1)
    step_id = pl.program_id(0) * pl.num_programs(1) + step
    sample = _phase_sample_gla(step_id, sample_per_step, sq_ref, sk_ref, se_ref, sv_ref, sgla_in_ref, sgla_out_ref,
                               so_ref)

    @pl.when(step == 0)
    def _():
        xl_scr[0:SUBLANES, :] = xlc0_ref[...]
        hcar_scr[...] = h0_ref[...]
        scat_scr[...] = s0_ref[...]

    xl_keep[...] = xl_scr[0:SUBLANES, :]
    hcar_keep[...] = hcar_scr[...]
    scat_keep[...] = scat_scr[...]
    unsafe_ref[0] = 0

    n_sub = PROMPT_TILE // PROMPT_SUBTILE
    tiles = [types.SimpleNamespace(rows=PROMPT_SUBTILE) for _ in range(n_sub)]
    tri = tri_ref[...]

    def rows(s):
        return slice(s * PROMPT_SUBTILE, (s + 1) * PROMPT_SUBTILE)

    def project(s):
        return _phase_project(tiles[s], lambda: x_ref[0, rows(s), :], p)

    def ffn(s):
        def store(v):
            y_ref[0, rows(s), :] = v
        return _phase_ffn(tiles[s], fp, store)

    n_up = D_FF // MXU_DIM

    def after_mixers(s, ffn_rest):
        out_steps, ffn_steps = _phase_out(tiles[s], p), ffn(s)
        half = len(ffn_rest) // 2
        return (out_steps[:-1] + ffn_rest[:half] + out_steps[-1:] + ffn_rest[half:] + ffn_steps[:n_up],
                ffn_steps[n_up:])

    first = project(0)
    lru_ready = 2
    _run_interleaved(first[:lru_ready])
    pending, ffn_rest = [], []
    for s in range(n_sub):
        mixers = (_phase_lru(tiles[s], p, xl_scr, hcar_scr)
                  + _phase_gla(tiles[s], p, GLA_CHUNK, scat_scr, tri, unsafe_ref=unsafe_ref))
        others = (first[lru_ready:] if s == 0 else []) + (project(s + 1) if s + 1 < n_sub else [])
        _run_interleaved(mixers, *([others] if others else []), *([pending] if pending else []))
        pending, ffn_rest = after_mixers(s, ffn_rest)
    _run_interleaved(pending + ffn_rest, sample)

    @pl.when(unsafe_ref[0] != 0)
    def _():
        xl_scr[0:SUBLANES, :] = xl_keep[...]
        hcar_scr[...] = hcar_keep[...]
        scat_scr[...] = scat_keep[...]

        def redo(c, carry):
            chunk_rows = pl.ds(pl.multiple_of(c * FALLBACK_ROWS, FALLBACK_ROWS), FALLBACK_ROWS)
            t = types.SimpleNamespace(rows=FALLBACK_ROWS)

            def store(v):
                y_ref[0, chunk_rows, :] = v

            _run_interleaved(_phase_project(t, lambda: x_ref[0, chunk_rows, :], p)
                             + _phase_lru(t, p, xl_scr, hcar_scr)
                             + _phase_gla(t, p, FALLBACK_ROWS, scat_scr, safe=True)
                             + _phase_out(t, p) + _phase_ffn(t, fp, store))
            return carry

        lax.fori_loop(0, PROMPT_TILE // FALLBACK_ROWS, redo, 0)

    @pl.when(step == pl.num_programs(1) - 1)
    def _():
        gla_ref[0] = scat_scr[...]
        lru_ref[0] = hcar_scr[...]
        conv_ref[0] = xl_scr[SUBLANES - (CONV_K - 1):SUBLANES, :]


def _post_kernel(hln_ref, ylru_ref, gg_ref, o_ref, gn_g_ref, w_out_ref, ln1_g_ref, ln1_b_ref, *refs):
    fp = FfnParams(*refs[:N_FFN_PARAMS])
    out_ref, = refs[N_FFN_PARAMS:]
    y_gla = _gla_finish(o_ref[...], gg_ref[...], gn_g_ref[...])
    y = jnp.concatenate([ylru_ref[...], y_gla], axis=1).astype(bf16)
    h1 = _layer_norm(DEEPNORM_ALPHA * hln_ref[...] + _dot(y, w_out_ref[...]), ln1_g_ref[...], ln1_b_ref[...])
    t = types.SimpleNamespace(h1=h1, h1b=h1.astype(bf16))

    def store(v):
        out_ref[:, 0, :] = v

    _run_interleaved(_phase_ffn(t, fp, store))


def _block_diag(blocks):
    n, r, c = blocks.shape
    eye = jnp.eye(n, dtype=blocks.dtype)
    return (eye[:, None, :, None] * blocks[:, :, None, :]).reshape(n * r, n * c)


def _const_spec(arr):
    zeros = (0,) * arr.ndim
    return pl.BlockSpec(arr.shape, lambda *_: zeros, pipeline_mode=pl.Buffered(1))


def _mixer_params(ln_in_g, ln_in_b, w_in, conv_w, conv_b, ga_w, ga_b, gx_w, gx_b, lam, al_w, al_b, gn_g, w_out,
                  ln1_g, ln1_b):
    row = lambda a: a.reshape(1, -1).astype(f32)
    w_in_p = jnp.pad(w_in, ((0, 0), (0, IN_COLS_PAD - IN_COLS))).astype(bf16)
    per_half = MXU_DIM // LRU_BLK
    w_gate = jnp.stack([
        jnp.concatenate([_block_diag(ga_w[c * per_half:(c + 1) * per_half]),
                         _block_diag(gx_w[c * per_half:(c + 1) * per_half])], axis=1)
        for c in range(LRU_W // MXU_DIM)]).astype(bf16)
    al_w_p = jnp.pad(al_w, ((0, ALR_PAD - GLA_RANK), (0, 0))).astype(bf16)
    return MixerParams(row(ln_in_g), row(ln_in_b), w_in_p, conv_w.astype(f32), row(conv_b), w_gate, row(ga_b),
                       row(gx_b), row(lam), al_w_p, row(al_b), row(gn_g), w_out.astype(bf16), row(ln1_g), row(ln1_b))


def _pre_call(meta_tokens, x_sample, conv_in, lru_in, params):
    n = x_sample.shape[0]
    sds = lambda *shape: jax.ShapeDtypeStruct(shape, f32)
    out_shape = (sds(SUBLANES, LRU_W), sds(1, LRU_W), sds(GLA_DK, GLA_DVH),
                 sds(n, LRU_W), sds(CONV_K - 1, n, LRU_W),
                 sds(n, D_MODEL), sds(n, LRU_W),
                 sds(n, GLA_DK), sds(n, GLA_DK), sds(n, GLA_DK), sds(n, GLA_DV), sds(n, GLA_DV))
    return pl.pallas_call(
        _pre_kernel, out_shape=out_shape, name="meta_and_sample_pre",
        scratch_shapes=[pltpu.VMEM((N_META + SUBLANES, LRU_W), f32)],
        compiler_params=pltpu.CompilerParams(vmem_limit_bytes=VMEM_LIMIT_BYTES),
    )(meta_tokens, x_sample, conv_in, lru_in, *params)


def _prompt_call(x, xlc0, h0, s0, sample_q, sample_k, sample_decay, sample_v, sample_gla, params, ffn_params):
    batch, seq, _ = x.shape
    assert seq % PROMPT_TILE == 0 and PROMPT_TILE % PROMPT_SUBTILE == 0 and PROMPT_SUBTILE % GLA_CHUNK == 0
    n_tiles = seq // PROMPT_TILE
    n_steps = batch * n_tiles
    n_sample = sample_q.shape[0]
    per_step = n_sample // n_steps
    assert per_step * n_steps == n_sample and SUBLANES % per_step == 0 and n_sample % SUBLANES == 0
    groups = SUBLANES // per_step
    pos = jnp.arange(PROMPT_SUBTILE)
    tri = ((pos[:, None] >= pos[None, :])
           & (pos[:, None] // GLA_CHUNK == pos[None, :] // GLA_CHUNK)).astype(bf16)
    seq_map = lambda b, t: (b, t, 0)
    state_map = lambda b, t: (b, 0, 0)
    rows8_map = lambda b, t: ((b * n_tiles + t) // groups, 0, 0)
    step_map3 = lambda b, t: (b * n_tiles + t, 0, 0)
    step_map4 = lambda b, t: (b * n_tiles + t, 0, 0, 0)
    rows8 = lambda a: a.reshape(n_sample // SUBLANES, SUBLANES, a.shape[-1])
    sample_state_block = (per_step, GLA_H, GLA_DKH, GLA_DVH)
    out_shape = (jax.ShapeDtypeStruct((batch, seq, D_MODEL), f32),
                 jax.ShapeDtypeStruct((batch, GLA_DK, GLA_DVH), f32),
                 jax.ShapeDtypeStruct((batch, 1, LRU_W), f32),
                 jax.ShapeDtypeStruct((batch, CONV_K - 1, LRU_W), f32),
                 jax.ShapeDtypeStruct(sample_gla.shape, f32),
                 jax.ShapeDtypeStruct((n_steps, per_step, GLA_DV), f32))
    y, gla, lru, conv, sample_gla_new, sample_o = pl.pallas_call(
        functools.partial(_prompt_kernel, per_step), out_shape=out_shape, name="prompt_layer",
        grid=(batch, n_tiles),
        in_specs=[pl.BlockSpec((1, PROMPT_TILE, D_MODEL), seq_map),
                  _const_spec(xlc0), _const_spec(h0), _const_spec(s0), _const_spec(tri),
                  pl.BlockSpec((1, SUBLANES, GLA_DK), rows8_map), pl.BlockSpec((1, SUBLANES, GLA_DK), rows8_map),
                  pl.BlockSpec((1, SUBLANES, GLA_DK), rows8_map), pl.BlockSpec((1, SUBLANES, GLA_DV), rows8_map),
                  pl.BlockSpec(sample_state_block, step_map4)]
                 + [_const_spec(a) for a in params] + [_const_spec(a) for a in ffn_params],
        out_specs=(pl.BlockSpec((1, PROMPT_TILE, D_MODEL), seq_map),
                   pl.BlockSpec((1, GLA_DK, GLA_DVH), state_map),
                   pl.BlockSpec((1, 1, LRU_W), state_map),
                   pl.BlockSpec((1, CONV_K - 1, LRU_W), state_map),
                   pl.BlockSpec(sample_state_block, step_map4),
                   pl.BlockSpec((1, per_step, GLA_DV), step_map3)),
        scratch_shapes=[pltpu.VMEM((PROMPT_SUBTILE + SUBLANES, LRU_W), f32),
                        pltpu.VMEM((1, LRU_W), f32),
                        pltpu.VMEM((GLA_DK, GLA_DVH), f32),
                        pltpu.VMEM((SUBLANES, LRU_W), f32),
                        pltpu.VMEM((1, LRU_W), f32),
                        pltpu.VMEM((GLA_DK, GLA_DVH), f32),
                        pltpu.SMEM((1,), jnp.int32)],
        compiler_params=pltpu.CompilerParams(dimension_semantics=("arbitrary", "arbitrary"),
                                             vmem_limit_bytes=VMEM_LIMIT_BYTES),
    )(x, xlc0, h0, s0, tri, rows8(sample_q), rows8(sample_k), rows8(sample_decay), rows8(sample_v), sample_gla,
      *params, *ffn_params)
    return (y, gla, lru, conv), (sample_gla_new, sample_o.reshape(n_sample, GLA_DV))


def _post_call(hln, ylru, gate, o, params, ffn_params):
    return pl.pallas_call(
        _post_kernel, out_shape=jax.ShapeDtypeStruct((hln.shape[0], 1, D_MODEL), f32), name="sample_post",
        compiler_params=pltpu.CompilerParams(vmem_limit_bytes=VMEM_LIMIT_BYTES),
    )(hln, ylru, gate, o, params.gn_g, params.w_out, params.ln1_g, params.ln1_b, *ffn_params)


def kernel(x_prompt, x_sample, state_gla, state_lru, state_conv, meta_tokens, ln_in_g, ln_in_b, w_in, conv_w, conv_b, lru_gate_a_w, lru_gate_a_b, lru_gate_x_w, lru_gate_x_b, lru_lambda, gla_alpha_w, gla_alpha_b, gla_norm_g, w_out, ln1_g, ln1_b, w_ffn_gate, w_ffn_up, w_ffn_down, ln2_g, ln2_b):
    assert w_in.shape[0] == 1 and x_sample.shape[1] == 1, "one layer, one decode token per sample sequence"
    batch, seq, _ = x_prompt.shape
    n_sample = x_sample.shape[0]
    params = _mixer_params(ln_in_g, ln_in_b, w_in[0], conv_w[0], conv_b[0], lru_gate_a_w[0], lru_gate_a_b[0],
                           lru_gate_x_w[0], lru_gate_x_b[0], lru_lambda[0], gla_alpha_w[0], gla_alpha_b[0],
                           gla_norm_g[0], w_out[0], ln1_g[0], ln1_b[0])
    ffn_params = FfnParams(w_ffn_gate[0].astype(bf16), w_ffn_up[0].astype(bf16), w_ffn_down[0].astype(bf16),
                           ln2_g[0].reshape(1, -1), ln2_b[0].reshape(1, -1))

    (xlc0, h0, s0, lru_s, conv_s, hln_s, ylru_s, q_s, k_s, decay_s, v_s, gate_s) = _pre_call(
        meta_tokens, x_sample, jnp.swapaxes(state_conv[0], 0, 1), state_lru[0], params)
    (y_p, gla_p, lru_p, conv_p), (gla_s, o_s) = _prompt_call(
        x_prompt, xlc0, h0, s0, q_s, k_s, decay_s, v_s, state_gla[0], params, ffn_params)
    y_s = _post_call(hln_s, ylru_s, gate_s, o_s, params, ffn_params)
    return (y_p,
            y_s,
            gla_p.reshape(1, batch, GLA_H, GLA_DKH, GLA_DVH),
            lru_p.reshape(1, batch, LRU_W),
            conv_p[None],
            gla_s[None],
            lru_s[None],
            jnp.swapaxes(conv_s, 0, 1)[None])
```

```python
import functools
import math
import types
from typing import NamedTuple

import jax
import jax.numpy as jnp
from jax import lax
from jax.experimental import pallas as pl
from jax.experimental.pallas import tpu as pltpu

f32 = jnp.float32
bf16 = jnp.bfloat16

D_MODEL = 1024
N_META = 16
LRU_W = 512
LRU_BLOCKS = 8
LRU_BLK = LRU_W // LRU_BLOCKS
LRU_C = 8.0
CONV_K = 4
GLA_H = 4
GLA_DKH = 64
GLA_DVH = 128
GLA_DK = GLA_H * GLA_DKH
GLA_DV = GLA_H * GLA_DVH
GLA_RANK = 16
GLA_TAU = 16.0
D_FF = 2816
DEEPNORM_ALPHA = 2.0 ** 0.25
LN_EPS = 1e-5
RMS_EPS = 1e-6
GELU_C = math.sqrt(2.0 / math.pi)
GELU_CUBIC = 0.044715
LOG2_E = math.log2(math.e)
F32_TINY = float(jnp.finfo(jnp.float32).tiny)

COL_XLRU = 0
COL_GLRU = COL_XLRU + LRU_W
COL_Q = COL_GLRU + LRU_W
COL_K = COL_Q + GLA_DK
COL_V = COL_K + GLA_DK
COL_GGLA = COL_V + GLA_DV
COL_ALR = COL_GGLA + GLA_DV
IN_COLS = COL_ALR + GLA_RANK

LANES = 128
SUBLANES = 8
MXU_DIM = 256
WIDE_STEP_COLS = 2 * MXU_DIM
ALR_PAD = LANES
IN_COLS_PAD = COL_ALR + ALR_PAD
VMEM_LIMIT_BYTES = 56 * 1024 * 1024

PROMPT_TILE = 512
PROMPT_SUBTILE = 256
GLA_CHUNK = 128
FALLBACK_ROWS = 64
GLA_SAFE_LOG_DECAY = 80.0


class MixerParams(NamedTuple):
    ln_g: jax.Array
    ln_b: jax.Array
    w_in: jax.Array
    conv_w: jax.Array
    conv_b: jax.Array
    w_gate: jax.Array
    gate_a_b: jax.Array
    gate_x_b: jax.Array
    lam: jax.Array
    al_w: jax.Array
    al_b: jax.Array
    gn_g: jax.Array
    w_out: jax.Array
    ln1_g: jax.Array
    ln1_b: jax.Array


N_MIXER_PARAMS = len(MixerParams._fields)


class FfnParams(NamedTuple):
    w_gate: jax.Array
    w_up: jax.Array
    w_down: jax.Array
    ln2_g: jax.Array
    ln2_b: jax.Array


N_FFN_PARAMS = len(FfnParams._fields)


def _layer_norm(x, g, b):
    mu = jnp.mean(x, -1, keepdims=True)
    xc = x - mu
    var = jnp.mean(xc * xc, -1, keepdims=True)
    return xc * lax.rsqrt(var + LN_EPS) * g + b


def _dot(a, b):
    return jnp.dot(a, b, preferred_element_type=f32)


def _dot_nt(a, b):
    return lax.dot_general(a, b, (((1,), (1,)), ((), ())), preferred_element_type=f32)


def _dot_tn(a, b):
    return lax.dot_general(a, b, (((0,), (0,)), ((), ())), preferred_element_type=f32)


def _sqrt_nonneg(y):
    return y * lax.rsqrt(jnp.maximum(y, F32_TINY))


def _gelu_tanh(x):
    k = -2.0 * GELU_C * LOG2_E
    return x * (1.0 / (1.0 + jnp.exp2(x * (k + (k * GELU_CUBIC) * (x * x)))))


def _gla_log_decay(z):
    scale = 1.0 / GLA_TAU
    return (jnp.minimum(z, 0.0) * scale
            - jnp.log2(1.0 + jnp.exp2(jnp.abs(z) * (-LOG2_E))) * (scale / LOG2_E))


def _lru_gates(u_half, half, p):
    lo = half * MXU_DIM
    hi = lo + MXU_DIM
    gates = _dot(u_half.astype(bf16), p.w_gate[half])
    r = jax.nn.sigmoid(gates[:, :MXU_DIM] + p.gate_a_b[:, lo:hi])
    i = jax.nn.sigmoid(gates[:, MXU_DIM:] + p.gate_x_b[:, lo:hi])
    neg_log_a = r * (LRU_C * jax.nn.softplus(-p.lam[:, lo:hi]))
    a = jnp.exp2(neg_log_a * (-LOG2_E))
    gain_sq = jnp.tanh(neg_log_a) * (a * a + 1.0)
    return a, _sqrt_nonneg(gain_sq) * (i * u_half)


def _scan_groups(a, x):
    rows, width = a.shape
    groups = rows // SUBLANES
    a3 = a.reshape(groups, SUBLANES, width)
    x3 = x.reshape(groups, SUBLANES, width)
    sub = lax.broadcasted_iota(jnp.int32, (groups, SUBLANES, width), 1)
    shift = 1
    while shift < SUBLANES:
        keep = sub >= shift
        a_prev = jnp.where(keep, pltpu.roll(a3, shift, 1), 1.0)
        x_prev = jnp.where(keep, pltpu.roll(x3, shift, 1), 0.0)
        x3 = x3 + a3 * x_prev
        a3 = a3 * a_prev
        shift *= 2
    return a3.reshape(rows, width), x3.reshape(rows, width)


def _lru_scan(a, x, h_in):
    rows = a.shape[0]
    big_a, big_x = _scan_groups(a, x)
    carry = h_in
    out = []
    for g in range(rows // SUBLANES):
        sl = slice(g * SUBLANES, (g + 1) * SUBLANES)
        hg = big_a[sl] * carry + big_x[sl]
        carry = hg[SUBLANES - 1:SUBLANES]
        out.append(hg)
    return jnp.concatenate(out, axis=0), carry


def _chunk_cumsum(g, chunk):
    row = lax.broadcasted_iota(jnp.int32, g.shape, 0) % chunk
    shift = 1
    while shift < chunk:
        g = g + jnp.where(row >= shift, pltpu.roll(g, shift, 0), 0.0)
        shift *= 2
    return g


def _chunk_cumsum_mxu(g, tri):
    hi = g.astype(bf16)
    lo = (g - hi.astype(f32)).astype(bf16)
    return _dot(tri, hi) + _dot(tri, lo)


def _head_stack(x, head_masks):
    zero = jnp.zeros_like(x)
    return jnp.concatenate([jnp.where(m, x, zero) for m in head_masks], axis=0)


def _gla_finish(o, g_gla, gn_g):
    heads = []
    for h in range(GLA_H):
        oh = o[:, h * GLA_DVH:(h + 1) * GLA_DVH]
        heads.append(oh * lax.rsqrt(jnp.mean(oh * oh, -1, keepdims=True) + RMS_EPS))
    return jnp.concatenate(heads, axis=1) * gn_g * jax.nn.silu(g_gla)


STEP_COST = dict(
    in_norm=0.6, in_lru=1.0, in_qk=0.5, in_v=0.5, in_g=0.6,
    conv=0.9, gates=0.7, scan=0.8, lru_finish=0.5,
    gla_prepare=0.9, gla_chunk=0.4, gla_finish=0.5,
    out_cols=0.5, out_norm=0.5,
    ffn_up=0.5, ffn_down=1.4, ffn_norm=0.5,
    sample_load=0.05, sample_sequence=0.1,
)


def _phase_project(t, x_of, p):
    def norm():
        t.h = _layer_norm(x_of(), p.ln_g[...], p.ln_b[...])
        t.hb = t.h.astype(bf16)

    def lru():
        t.p_lru = _dot(t.hb, p.w_in[:, COL_XLRU:COL_Q])

    def qk():
        t.p_qk = _dot(t.hb, p.w_in[:, COL_Q:COL_V])

    def v():
        t.p_v = _dot(t.hb, p.w_in[:, COL_V:COL_GGLA])

    def g():
        t.p_g = _dot(t.hb, p.w_in[:, COL_GGLA:IN_COLS_PAD])

    c = STEP_COST
    return [(norm, c["in_norm"]), (lru, c["in_lru"]), (qk, c["in_qk"]), (v, c["in_v"]), (g, c["in_g"])]


def _phase_lru(t, p, xl_ref, hcar_ref, want_out=True):
    halves = range(LRU_W // MXU_DIM)
    t.hs, t.carry, t.gates = {}, {}, {}

    def conv():
        x_lru = t.p_lru[:, :LRU_W]
        rows = x_lru.shape[0]
        xl_ref[SUBLANES:SUBLANES + rows, :] = x_lru
        cw = p.conv_w[...]
        u = p.conv_b[...] + cw[CONV_K - 1:CONV_K] * x_lru
        for j in range(CONV_K - 1):
            start = SUBLANES - (CONV_K - 1) + j
            u = u + cw[j:j + 1] * xl_ref[start:start + rows, :]
        xl_ref[0:SUBLANES, :] = xl_ref[rows:rows + SUBLANES, :]
        t.u = u
        t.h_in = hcar_ref[...]

    def gates(half):
        def step():
            t.gates[half] = _lru_gates(t.u[:, half * MXU_DIM:(half + 1) * MXU_DIM], half, p)
        return step

    def scan(half):
        def step():
            a, xin = t.gates[half]
            t.hs[half], t.carry[half] = _lru_scan(a, xin, t.h_in[:, half * MXU_DIM:(half + 1) * MXU_DIM])
        return step

    def finish():
        hcar_ref[...] = jnp.concatenate([t.carry[h] for h in halves], axis=1)
        if want_out:
            hs = jnp.concatenate([t.hs[h] for h in halves], axis=1)
            t.y_lru = (hs * _gelu_tanh(t.p_lru[:, LRU_W:])).astype(bf16)

    steps = [(conv, STEP_COST["conv"])]
    for half in halves:
        steps += [(gates(half), STEP_COST["gates"]), (scan(half), STEP_COST["scan"])]
    return steps + [(finish, STEP_COST["lru_finish"])]


def _safe_scores(qs, k, b, head_masks):
    chunk = qs.shape[0]
    t_idx = lax.broadcasted_iota(jnp.int32, (GLA_H * chunk, chunk), 0) % chunk
    s_idx = lax.broadcasted_iota(jnp.int32, (GLA_H * chunk, chunk), 1)
    row = lax.broadcasted_iota(jnp.int32, (chunk, chunk), 0)
    col = lax.broadcasted_iota(jnp.int32, (chunk, chunk), 1)
    row_wide = lax.broadcasted_iota(jnp.int32, (chunk, GLA_DK), 0)
    diag = _dot_nt(_head_stack(qs.astype(bf16), head_masks), k.astype(bf16))
    total = jnp.where(t_idx == s_idx, diag, 0.0)
    size = 2
    while size <= chunk:
        half = size // 2
        split_row = (row // size) * size + (half - 1)
        pick = (col == split_row).astype(f32)
        b_split = jnp.dot(pick, b, preferred_element_type=f32, precision=lax.Precision.HIGHEST)
        decay = jnp.exp(-jnp.abs(b - b_split))
        second = (row_wide % size) >= half
        q_part = jnp.where(second, qs * decay, 0.0).astype(bf16)
        k_part = jnp.where(second, 0.0, k * decay).astype(bf16)
        part = _dot_nt(_head_stack(q_part, head_masks), k_part)
        total = total + jnp.where((t_idx // size) == (s_idx // size), part, 0.0)
        size *= 2
    return total


def _phase_gla(t, p, chunk, scat_ref, tri=None, want_out=True, safe=False, unsafe_ref=None):
    outs = []

    def prepare():
        q, k = t.p_qk[:, :GLA_DK], t.p_qk[:, GLA_DK:]
        n_chunks = q.shape[0] // chunk
        z = _dot(t.p_g[:, GLA_DV:].astype(bf16), p.al_w[...]) + p.al_b[...]
        g = _gla_log_decay(z)
        b = _chunk_cumsum(g, chunk) if tri is None else _chunk_cumsum_mxu(g, tri)
        last_rows = [b[(c + 1) * chunk - 1:(c + 1) * chunk] for c in range(n_chunks)]
        b_last = jnp.concatenate([jnp.broadcast_to(r, (chunk, GLA_DK)) for r in last_rows], axis=0)
        qs = q * (GLA_DKH ** -0.5)
        t.q_state = (qs * jnp.exp(b)).astype(bf16)
        t.k_end = (k * jnp.exp(b_last - b)).astype(bf16)
        if safe:
            t.qs, t.k, t.b = qs, k, b
        else:
            t.q_end = (qs * jnp.exp(b - b_last)).astype(bf16)
            if unsafe_ref is not None:
                strongest = jnp.min(jnp.concatenate(last_rows, axis=0))
                unsafe_ref[0] = jnp.where(strongest < -GLA_SAFE_LOG_DECAY, 1, unsafe_ref[0])
        t.vb = t.p_v.astype(bf16)
        pad = jnp.zeros((SUBLANES - n_chunks, GLA_DK), f32)
        t.chunk_decay = jnp.exp(jnp.concatenate(last_rows + [pad], axis=0)).T
        lane = lax.broadcasted_iota(jnp.int32, (chunk, GLA_DK), 1)
        t.head_masks = [(lane // GLA_DKH) == h for h in range(GLA_H)]
        t_idx = lax.broadcasted_iota(jnp.int32, (GLA_H * chunk, chunk), 0) % chunk
        s_idx = lax.broadcasted_iota(jnp.int32, (GLA_H * chunk, chunk), 1)
        t.causal = s_idx <= t_idx

    def one_chunk(c):
        def step():
            sl = slice(c * chunk, (c + 1) * chunk)
            if safe:
                probs = _safe_scores(t.qs[sl], t.k[sl], t.b[sl], t.head_masks).astype(bf16)
            else:
                scores = _dot_nt(_head_stack(t.q_end[sl], t.head_masks), t.k_end[sl])
                probs = jnp.where(t.causal, scores, 0.0).astype(bf16)
            state = scat_ref[...]
            o_state = _dot(_head_stack(t.q_state[sl], t.head_masks), state.astype(bf16))
            v_heads = [t.vb[sl, h * GLA_DVH:(h + 1) * GLA_DVH] for h in range(GLA_H)]
            outs.append(jnp.concatenate(
                [o_state[h * chunk:(h + 1) * chunk] + _dot(probs[h * chunk:(h + 1) * chunk], v_heads[h])
                 for h in range(GLA_H)], axis=1))
            d_state = _dot_tn(_head_stack(t.k_end[sl], t.head_masks), jnp.concatenate(v_heads, axis=0))
            scat_ref[...] = t.chunk_decay[:, c:c + 1] * state + d_state
        return step

    def finish():
        if want_out:
            t.y_gla = _gla_finish(jnp.concatenate(outs, axis=0), t.p_g[:, :GLA_DV], p.gn_g[...]).astype(bf16)

    return ([(prepare, STEP_COST["gla_prepare"])]
            + [(one_chunk(c), STEP_COST["gla_chunk"]) for c in range(t.rows // chunk)]
            + [(finish, STEP_COST["gla_finish"])])


def _phase_out(t, p):
    mix = []

    def project(j):
        def step():
            cols = slice(j * WIDE_STEP_COLS, (j + 1) * WIDE_STEP_COLS)
            mix.append(_dot(t.y_lru, p.w_out[0:LRU_W, cols]) + _dot(t.y_gla, p.w_out[LRU_W:, cols]))
        return step

    def norm():
        t.h1 = _layer_norm(DEEPNORM_ALPHA * t.h + jnp.concatenate(mix, axis=1), p.ln1_g[...], p.ln1_b[...])
        t.h1b = t.h1.astype(bf16)

    return ([(project(j), STEP_COST["out_cols"]) for j in range(D_MODEL // WIDE_STEP_COLS)]
            + [(norm, STEP_COST["out_norm"])])


def _phase_ffn(t, fp, store):
    acts, outs = [], []

    def up(n):
        def step():
            cols = slice(n * MXU_DIM, (n + 1) * MXU_DIM)
            gate = _dot(t.h1b, fp.w_gate[:, cols])
            acts.append((jax.nn.silu(gate) * _dot(t.h1b, fp.w_up[:, cols])).astype(bf16))
        return step

    def down(j):
        def step():
            if j == 0:
                t.act = jnp.concatenate(acts, axis=1)
            outs.append(_dot(t.act, fp.w_down[:, j * WIDE_STEP_COLS:(j + 1) * WIDE_STEP_COLS]))
        return step

    def norm():
        ffn = jnp.concatenate(outs, axis=1)
        store(_layer_norm(DEEPNORM_ALPHA * t.h1 + ffn, fp.ln2_g[...], fp.ln2_b[...]))

    return ([(up(n), STEP_COST["ffn_up"]) for n in range(D_FF // MXU_DIM)]
            + [(down(j), STEP_COST["ffn_down"]) for j in range(D_MODEL // WIDE_STEP_COLS)]
            + [(norm, STEP_COST["ffn_norm"])])


def _run_interleaved(*phases):
    keyed = []
    for n, steps in enumerate(phases):
        total = sum(cost for _, cost in steps)
        done = 0.0
        for step, cost in steps:
            keyed.append(((done + 0.5 * cost) / total, n, step))
            done += cost
    for _, _, step in sorted(keyed, key=lambda e: e[:2]):
        step()


def _pre_kernel(meta_ref, x_ref, conv_in_ref, lru_in_ref, *refs):
    p = MixerParams(*refs[:N_MIXER_PARAMS])
    (xlc_ref, h_ref, s_ref, lru_out_ref, conv_out_ref, hln_ref, ylru_ref, q_ref, k_ref, eg_ref, v_ref, gg_ref,
     xl_scr) = refs[N_MIXER_PARAMS:]
    xl_scr[...] = jnp.zeros_like(xl_scr)
    h_ref[...] = jnp.zeros_like(h_ref)
    s_ref[...] = jnp.zeros_like(s_ref)
    t = types.SimpleNamespace(rows=N_META)
    _run_interleaved(_phase_project(t, lambda: meta_ref[...], p))
    _run_interleaved(_phase_lru(t, p, xl_scr, h_ref, want_out=False))
    _run_interleaved(_phase_gla(t, p, N_META, s_ref, want_out=False, safe=True))
    xlc_ref[...] = xl_scr[0:SUBLANES, :]

    h = _layer_norm(x_ref[:, 0, :], p.ln_g[...], p.ln_b[...])
    hln_ref[...] = h
    proj = _dot(h.astype(bf16), p.w_in[...])
    x_lru = proj[:, COL_XLRU:COL_GLRU]
    cw = p.conv_w[...]
    u = p.conv_b[...] + cw[CONV_K - 1:CONV_K] * x_lru
    for j in range(CONV_K - 1):
        u = u + cw[j:j + 1] * conv_in_ref[j]
    for j in range(CONV_K - 2):
        conv_out_ref[j] = conv_in_ref[j + 1]
    conv_out_ref[CONV_K - 2] = x_lru
    h_in = lru_in_ref[...]
    halves = []
    for half in range(LRU_W // MXU_DIM):
        lo, hi = half * MXU_DIM, (half + 1) * MXU_DIM
        a, xin = _lru_gates(u[:, lo:hi], half, p)
        halves.append(a * h_in[:, lo:hi] + xin)
    h_new = jnp.concatenate(halves, axis=1)
    lru_out_ref[...] = h_new
    ylru_ref[...] = h_new * _gelu_tanh(proj[:, COL_GLRU:COL_Q])
    z = _dot(proj[:, COL_ALR:COL_ALR + ALR_PAD].astype(bf16), p.al_w[...]) + p.al_b[...]
    eg_ref[...] = jnp.exp(_gla_log_decay(z))
    q_ref[...] = proj[:, COL_Q:COL_K] * (GLA_DKH ** -0.5)
    k_ref[...] = proj[:, COL_K:COL_V]
    v_ref[...] = proj[:, COL_V:COL_GGLA]
    gg_ref[...] = proj[:, COL_GGLA:COL_ALR]


def _phase_sample_gla(step_id, per_step, q_ref, k_ref, e_ref, v_ref, s_in_ref, s_out_ref, o_ref):
    groups = SUBLANES // per_step
    st = types.SimpleNamespace()
    o_rows = []

    def load():
        group = step_id & (groups - 1)

        def mine(full, axis):
            parts = [lax.slice_in_dim(full, g * per_step, (g + 1) * per_step, axis=axis) for g in range(groups)]
            out = parts[0]
            for g in range(1, groups):
                out = jnp.where(group == g, parts[g], out)
            return out

        st.q, st.k, st.e = (mine(ref[0].T, 1) for ref in (q_ref, k_ref, e_ref))
        st.v = mine(v_ref[0], 0)

    def one_sequence(j):
        def step():
            heads = []
            for h in range(GLA_H):
                ks = slice(h * GLA_DKH, (h + 1) * GLA_DKH)
                s_new = (st.e[ks, j:j + 1] * s_in_ref[j, h]
                         + st.k[ks, j:j + 1] * st.v[j:j + 1, h * GLA_DVH:(h + 1) * GLA_DVH])
                s_out_ref[j, h] = s_new
                heads.append(jnp.sum(st.q[ks, j:j + 1] * s_new, axis=0, keepdims=True))
            o_rows.append(jnp.concatenate(heads, axis=1))
        return step

    def finish():
        o_ref[0] = jnp.concatenate(o_rows, axis=0)

    return ([(load, STEP_COST["sample_load"])]
            + [(one_sequence(j), STEP_COST["sample_sequence"]) for j in range(per_step)]
            + [(finish, STEP_COST["sample_load"])])


def _prompt_kernel(sample_per_step, x_ref, x_next_ref, xlc0_ref, h0_ref, s0_ref, tri_ref, sq_ref, sk_ref, se_ref,
                   sv_ref, sgla_in_ref, *refs):
    p = MixerParams(*refs[:N_MIXER_PARAMS])
    fp = FfnParams(*refs[N_MIXER_PARAMS:N_MIXER_PARAMS + N_FFN_PARAMS])
    (y_ref, gla_ref, lru_ref, conv_ref, sgla_out_ref, so_ref, xl_scr, hcar_scr, scat_scr,
     xl_keep, hcar_keep, scat_keep, unsafe_ref, normed_scr, normed_bf16_scr) = refs[N_MIXER_PARAMS + N_FFN_PARAMS:]
    step = pl.program_id(1)
    step_id = pl.program_id(0) * pl.num_programs(1) + step

    def hand_over_norm(x):
        h = _layer_norm(x, p.ln_g[...], p.ln_b[...])
        normed_scr[...] = h
        normed_bf16_scr[...] = h.astype(bf16)

    @pl.when(step_id == 0)
    def _():
        hand_over_norm(x_ref[0, 0:PROMPT_SUBTILE, :])
    sample = _phase_sample_gla(step_id, sample_per_step, sq_ref, sk_ref, se_ref, sv_ref, sgla_in_ref, sgla_out_ref,
                               so_ref)

    @pl.when(step == 0)
    def _():
        xl_scr[0:SUBLANES, :] = xlc0_ref[...]
        hcar_scr[...] = h0_ref[...]
        scat_scr[...] = s0_ref[...]

    xl_keep[...] = xl_scr[0:SUBLANES, :]
    hcar_keep[...] = hcar_scr[...]
    scat_keep[...] = scat_scr[...]
    unsafe_ref[0] = 0

    n_sub = PROMPT_TILE // PROMPT_SUBTILE
    tiles = [types.SimpleNamespace(rows=PROMPT_SUBTILE) for _ in range(n_sub)]
    tri = tri_ref[...]

    def rows(s):
        return slice(s * PROMPT_SUBTILE, (s + 1) * PROMPT_SUBTILE)

    def project(s):
        return _phase_project(tiles[s], lambda: x_ref[0, rows(s), :], p)

    def ffn(s):
        def store(v):
            y_ref[0, rows(s), :] = v
        return _phase_ffn(tiles[s], fp, store)

    n_up = D_FF // MXU_DIM

    def after_mixers(s, ffn_rest):
        out_steps, ffn_steps = _phase_out(tiles[s], p), ffn(s)
        half = len(ffn_rest) // 2
        return (out_steps[:-1] + ffn_rest[:half] + out_steps[-1:] + ffn_rest[half:] + ffn_steps[:n_up],
                ffn_steps[n_up:])

    tiles[0].h = normed_scr[...]
    tiles[0].hb = normed_bf16_scr[...]
    first = project(0)[1:]
    lru_ready = 1
    _run_interleaved(first[:lru_ready])
    pending, ffn_rest = [], []
    for s in range(n_sub):
        mixers = (_phase_lru(tiles[s], p, xl_scr, hcar_scr)
                  + _phase_gla(tiles[s], p, GLA_CHUNK, scat_scr, tri, unsafe_ref=unsafe_ref))
        others = (first[lru_ready:] if s == 0 else []) + (project(s + 1) if s + 1 < n_sub else [])
        _run_interleaved(mixers, *([others] if others else []), *([pending] if pending else []))
        pending, ffn_rest = after_mixers(s, ffn_rest)
    next_norm = [(lambda: hand_over_norm(x_next_ref[0]), STEP_COST["in_norm"])]
    _run_interleaved(pending + ffn_rest, sample + next_norm)

    @pl.when(unsafe_ref[0] != 0)
    def _():
        xl_scr[0:SUBLANES, :] = xl_keep[...]
        hcar_scr[...] = hcar_keep[...]
        scat_scr[...] = scat_keep[...]

        def redo(c, carry):
            chunk_rows = pl.ds(pl.multiple_of(c * FALLBACK_ROWS, FALLBACK_ROWS), FALLBACK_ROWS)
            t = types.SimpleNamespace(rows=FALLBACK_ROWS)

            def store(v):
                y_ref[0, chunk_rows, :] = v

            _run_interleaved(_phase_project(t, lambda: x_ref[0, chunk_rows, :], p)
                             + _phase_lru(t, p, xl_scr, hcar_scr)
                             + _phase_gla(t, p, FALLBACK_ROWS, scat_scr, safe=True)
                             + _phase_out(t, p) + _phase_ffn(t, fp, store))
            return carry

        lax.fori_loop(0, PROMPT_TILE // FALLBACK_ROWS, redo, 0)

    @pl.when(step == pl.num_programs(1) - 1)
    def _():
        gla_ref[0] = scat_scr[...]
        lru_ref[0] = hcar_scr[...]
        conv_ref[0] = xl_scr[SUBLANES - (CONV_K - 1):SUBLANES, :]


def _post_kernel(hln_ref, ylru_ref, gg_ref, o_ref, gn_g_ref, w_out_ref, ln1_g_ref, ln1_b_ref, *refs):
    fp = FfnParams(*refs[:N_FFN_PARAMS])
    out_ref, = refs[N_FFN_PARAMS:]
    y_gla = _gla_finish(o_ref[...], gg_ref[...], gn_g_ref[...])
    y = jnp.concatenate([ylru_ref[...], y_gla], axis=1).astype(bf16)
    h1 = _layer_norm(DEEPNORM_ALPHA * hln_ref[...] + _dot(y, w_out_ref[...]), ln1_g_ref[...], ln1_b_ref[...])
    t = types.SimpleNamespace(h1=h1, h1b=h1.astype(bf16))

    def store(v):
        out_ref[:, 0, :] = v

    _run_interleaved(_phase_ffn(t, fp, store))


def _block_diag(blocks):
    n, r, c = blocks.shape
    eye = jnp.eye(n, dtype=blocks.dtype)
    return (eye[:, None, :, None] * blocks[:, :, None, :]).reshape(n * r, n * c)


def _const_spec(arr):
    zeros = (0,) * arr.ndim
    return pl.BlockSpec(arr.shape, lambda *_: zeros, pipeline_mode=pl.Buffered(1))


def _mixer_params(ln_in_g, ln_in_b, w_in, conv_w, conv_b, ga_w, ga_b, gx_w, gx_b, lam, al_w, al_b, gn_g, w_out,
                  ln1_g, ln1_b):
    row = lambda a: a.reshape(1, -1).astype(f32)
    w_in_p = jnp.pad(w_in, ((0, 0), (0, IN_COLS_PAD - IN_COLS))).astype(bf16)
    per_half = MXU_DIM // LRU_BLK
    w_gate = jnp.stack([
        jnp.concatenate([_block_diag(ga_w[c * per_half:(c + 1) * per_half]),
                         _block_diag(gx_w[c * per_half:(c + 1) * per_half])], axis=1)
        for c in range(LRU_W // MXU_DIM)]).astype(bf16)
    al_w_p = jnp.pad(al_w, ((0, ALR_PAD - GLA_RANK), (0, 0))).astype(bf16)
    return MixerParams(row(ln_in_g), row(ln_in_b), w_in_p, conv_w.astype(f32), row(conv_b), w_gate, row(ga_b),
                       row(gx_b), row(lam), al_w_p, row(al_b), row(gn_g), w_out.astype(bf16), row(ln1_g), row(ln1_b))


def _pre_call(meta_tokens, x_sample, conv_in, lru_in, params):
    n = x_sample.shape[0]
    sds = lambda *shape: jax.ShapeDtypeStruct(shape, f32)
    out_shape = (sds(SUBLANES, LRU_W), sds(1, LRU_W), sds(GLA_DK, GLA_DVH),
                 sds(n, LRU_W), sds(CONV_K - 1, n, LRU_W),
                 sds(n, D_MODEL), sds(n, LRU_W),
                 sds(n, GLA_DK), sds(n, GLA_DK), sds(n, GLA_DK), sds(n, GLA_DV), sds(n, GLA_DV))
    return pl.pallas_call(
        _pre_kernel, out_shape=out_shape, name="meta_and_sample_pre",
        scratch_shapes=[pltpu.VMEM((N_META + SUBLANES, LRU_W), f32)],
        compiler_params=pltpu.CompilerParams(vmem_limit_bytes=VMEM_LIMIT_BYTES),
    )(meta_tokens, x_sample, conv_in, lru_in, *params)


def _prompt_call(x, xlc0, h0, s0, sample_q, sample_k, sample_decay, sample_v, sample_gla, params, ffn_params):
    batch, seq, _ = x.shape
    assert seq % PROMPT_TILE == 0 and PROMPT_TILE % PROMPT_SUBTILE == 0 and PROMPT_SUBTILE % GLA_CHUNK == 0
    n_tiles = seq // PROMPT_TILE
    n_steps = batch * n_tiles
    n_sample = sample_q.shape[0]
    per_step = n_sample // n_steps
    assert per_step * n_steps == n_sample and SUBLANES % per_step == 0 and n_sample % SUBLANES == 0
    groups = SUBLANES // per_step
    pos = jnp.arange(PROMPT_SUBTILE)
    tri = ((pos[:, None] >= pos[None, :])
           & (pos[:, None] // GLA_CHUNK == pos[None, :] // GLA_CHUNK)).astype(bf16)
    seq_map = lambda b, t: (b, t, 0)
    state_map = lambda b, t: (b, 0, 0)

    def next_first_subtile(b, t):
        same_seq = t + 1 < n_tiles
        last_seq = b + 1 >= batch
        b_next = jnp.where(same_seq | last_seq, b, b + 1)
        t_next = jnp.where(same_seq, t + 1, jnp.where(last_seq, t, 0))
        return (b_next, t_next * (PROMPT_TILE // PROMPT_SUBTILE), 0)

    group_shift = groups.bit_length() - 1
    rows8_map = lambda b, t: ((b * n_tiles + t) >> group_shift, 0, 0)
    step_map3 = lambda b, t: (b * n_tiles + t, 0, 0)
    step_map4 = lambda b, t: (b * n_tiles + t, 0, 0, 0)
    rows8 = lambda a: a.reshape(n_sample // SUBLANES, SUBLANES, a.shape[-1])
    sample_state_block = (per_step, GLA_H, GLA_DKH, GLA_DVH)
    out_shape = (jax.ShapeDtypeStruct((batch, seq, D_MODEL), f32),
                 jax.ShapeDtypeStruct((batch, GLA_DK, GLA_DVH), f32),
                 jax.ShapeDtypeStruct((batch, 1, LRU_W), f32),
                 jax.ShapeDtypeStruct((batch, CONV_K - 1, LRU_W), f32),
                 jax.ShapeDtypeStruct(sample_gla.shape, f32),
                 jax.ShapeDtypeStruct((n_steps, per_step, GLA_DV), f32))
    y, gla, lru, conv, sample_gla_new, sample_o = pl.pallas_call(
        functools.partial(_prompt_kernel, per_step), out_shape=out_shape, name="prompt_layer",
        grid=(batch, n_tiles),
        in_specs=[pl.BlockSpec((1, PROMPT_TILE, D_MODEL), seq_map),
                  pl.BlockSpec((1, PROMPT_SUBTILE, D_MODEL), next_first_subtile),
                  _const_spec(xlc0), _const_spec(h0), _const_spec(s0), _const_spec(tri),
                  pl.BlockSpec((1, SUBLANES, GLA_DK), rows8_map), pl.BlockSpec((1, SUBLANES, GLA_DK), rows8_map),
                  pl.BlockSpec((1, SUBLANES, GLA_DK), rows8_map), pl.BlockSpec((1, SUBLANES, GLA_DV), rows8_map),
                  pl.BlockSpec(sample_state_block, step_map4)]
                 + [_const_spec(a) for a in params] + [_const_spec(a) for a in ffn_params],
        out_specs=(pl.BlockSpec((1, PROMPT_TILE, D_MODEL), seq_map),
                   pl.BlockSpec((1, GLA_DK, GLA_DVH), state_map),
                   pl.BlockSpec((1, 1, LRU_W), state_map),
                   pl.BlockSpec((1, CONV_K - 1, LRU_W), state_map),
                   pl.BlockSpec(sample_state_block, step_map4),
                   pl.BlockSpec((1, per_step, GLA_DV), step_map3)),
        scratch_shapes=[pltpu.VMEM((PROMPT_SUBTILE + SUBLANES, LRU_W), f32),
                        pltpu.VMEM((1, LRU_W), f32),
                        pltpu.VMEM((GLA_DK, GLA_DVH), f32),
                        pltpu.VMEM((SUBLANES, LRU_W), f32),
                        pltpu.VMEM((1, LRU_W), f32),
                        pltpu.VMEM((GLA_DK, GLA_DVH), f32),
                        pltpu.SMEM((1,), jnp.int32),
                        pltpu.VMEM((PROMPT_SUBTILE, D_MODEL), f32),
                        pltpu.VMEM((PROMPT_SUBTILE, D_MODEL), bf16)],
        compiler_params=pltpu.CompilerParams(dimension_semantics=("arbitrary", "arbitrary"),
                                             vmem_limit_bytes=VMEM_LIMIT_BYTES),
    )(x, x, xlc0, h0, s0, tri, rows8(sample_q), rows8(sample_k), rows8(sample_decay), rows8(sample_v), sample_gla,
      *params, *ffn_params)
    return (y, gla, lru, conv), (sample_gla_new, sample_o.reshape(n_sample, GLA_DV))


def _post_call(hln, ylru, gate, o, params, ffn_params):
    return pl.pallas_call(
        _post_kernel, out_shape=jax.ShapeDtypeStruct((hln.shape[0], 1, D_MODEL), f32), name="sample_post",
        compiler_params=pltpu.CompilerParams(vmem_limit_bytes=VMEM_LIMIT_BYTES),
    )(hln, ylru, gate, o, params.gn_g, params.w_out, params.ln1_g, params.ln1_b, *ffn_params)


def kernel(x_prompt, x_sample, state_gla, state_lru, state_conv, meta_tokens, ln_in_g, ln_in_b, w_in, conv_w, conv_b, lru_gate_a_w, lru_gate_a_b, lru_gate_x_w, lru_gate_x_b, lru_lambda, gla_alpha_w, gla_alpha_b, gla_norm_g, w_out, ln1_g, ln1_b, w_ffn_gate, w_ffn_up, w_ffn_down, ln2_g, ln2_b):
    assert w_in.shape[0] == 1 and x_sample.shape[1] == 1, "one layer, one decode token per sample sequence"
    batch, seq, _ = x_prompt.shape
    n_sample = x_sample.shape[0]
    params = _mixer_params(ln_in_g, ln_in_b, w_in[0], conv_w[0], conv_b[0], lru_gate_a_w[0], lru_gate_a_b[0],
                           lru_gate_x_w[0], lru_gate_x_b[0], lru_lambda[0], gla_alpha_w[0], gla_alpha_b[0],
                           gla_norm_g[0], w_out[0], ln1_g[0], ln1_b[0])
    ffn_params = FfnParams(w_ffn_gate[0].astype(bf16), w_ffn_up[0].astype(bf16), w_ffn_down[0].astype(bf16),
                           ln2_g[0].reshape(1, -1), ln2_b[0].reshape(1, -1))

    (xlc0, h0, s0, lru_s, conv_s, hln_s, ylru_s, q_s, k_s, decay_s, v_s, gate_s) = _pre_call(
        meta_tokens, x_sample, jnp.swapaxes(state_conv[0], 0, 1), state_lru[0], params)
    (y_p, gla_p, lru_p, conv_p), (gla_s, o_s) = _prompt_call(
        x_prompt, xlc0, h0, s0, q_s, k_s, decay_s, v_s, state_gla[0], params, ffn_params)
    y_s = _post_call(hln_s, ylru_s, gate_s, o_s, params, ffn_params)
    return (y_p,
            y_s,
            gla_p.reshape(1, batch, GLA_H, GLA_DKH, GLA_DVH),
            lru_p.reshape(1, batch, LRU_W),
            conv_p[None],
            gla_s[None],
            lru_s[None],
            jnp.swapaxes(conv_s, 0, 1)[None])
```
